```python
import math
import jax, jax.numpy as jnp
from jax import lax
import numpy as np

D_MODEL = 1024
BATCH = 8
SEQ = 2048
DEPTH = 1

EPS = 1e-6
PLE_DIM = 256
S5_GROUP_CH = 16
S5_GROUPS = D_MODEL // 32
S5_WIDTH = S5_GROUPS * S5_GROUP_CH
S5_STATE = 64
LRU_HEAD_DIM = 64
LRU_WIDTH = D_MODEL
LRU_HEADS = LRU_WIDTH // LRU_HEAD_DIM
LRU_C = 8.0
CONV_WIDTH = 4
FFN_HIDDEN = -(-8 * D_MODEL // (3 * 256)) * 256
IN_COLS = S5_WIDTH + LRU_WIDTH + 2 * D_MODEL

kernel_name = "hybrid_s5_rglru_gated_block"


def rms_norm(x, g):
    xf = x.astype(jnp.float32)
    y = xf * lax.rsqrt(jnp.mean(xf * xf, axis=-1, keepdims=True) + EPS)
    return (y * g.astype(jnp.float32)).astype(x.dtype)


def s5_mixer(u, lam_re, lam_im, log_dt, b_re, b_im, c_re, c_im, d_skip, w_glu, b_glu):
    f32 = jnp.float32
    bsz, L, _ = u.shape
    uf = u.astype(f32).reshape(bsz, L, S5_GROUPS, S5_GROUP_CH)
    lr = lam_re.astype(f32)
    li = lam_im.astype(f32)
    dt = jnp.exp(log_dt.astype(f32))[:, None]
    mag = jnp.exp(lr * dt)
    ar = mag * jnp.cos(li * dt)
    ai = mag * jnp.sin(li * dt)
    den = lr * lr + li * li
    nr = ar - 1.0
    fr = (nr * lr + ai * li) / den
    fi = (ai * lr - nr * li) / den
    br = b_re.astype(f32)
    bi = b_im.astype(f32)
    bbr = fr[..., None] * br - fi[..., None] * bi
    bbi = fr[..., None] * bi + fi[..., None] * br
    xr = jnp.einsum('blgp,gnp->blgn', uf, bbr)
    xi = jnp.einsum('blgp,gnp->blgn', uf, bbi)
    a_r = jnp.broadcast_to(ar, (1, L) + ar.shape)
    a_i = jnp.broadcast_to(ai, (1, L) + ai.shape)

    def combine(e1, e2):
        a1r, a1i, b1r, b1i = e1
        a2r, a2i, b2r, b2i = e2
        return (a2r * a1r - a2i * a1i,
                a2r * a1i + a2i * a1r,
                a2r * b1r - a2i * b1i + b2r,
                a2r * b1i + a2i * b1r + b2i)

    _, _, sr, si = lax.associative_scan(combine, (a_r, a_i, xr, xi), axis=1)
    y = (jnp.einsum('blgn,gpn->blgp', sr, c_re.astype(f32))
         - jnp.einsum('blgn,gpn->blgp', si, c_im.astype(f32))
         + d_skip.astype(f32) * uf)
    y = y.reshape(bsz, L, S5_WIDTH)
    z = jax.nn.gelu(y)
    out = z * jax.nn.sigmoid(z @ w_glu.astype(f32) + b_glu.astype(f32))
    return out.astype(u.dtype)


def rglru_mixer(u, conv_w, conv_b, w_r, b_r, w_i, b_i, lru_lambda):
    f32 = jnp.float32
    xc = lax.conv_general_dilated(
        u, conv_w[:, None, :].astype(u.dtype), window_strides=(1,),
        padding=[(CONV_WIDTH - 1, 0)], dimension_numbers=('NWC', 'WIO', 'NWC'),
        feature_group_count=LRU_WIDTH) + conv_b
    bsz, L, _ = xc.shape
    xh = xc.astype(f32).reshape(bsz, L, LRU_HEADS, LRU_HEAD_DIM)
    r = jax.nn.sigmoid(jnp.einsum('blhi,hij->blhj', xh, w_r.astype(f32)) + b_r.astype(f32))
    ig = jax.nn.sigmoid(jnp.einsum('blhi,hij->blhj', xh, w_i.astype(f32)) + b_i.astype(f32))
    log_a = -LRU_C * r * jax.nn.softplus(-lru_lambda.astype(f32).reshape(LRU_HEADS, LRU_HEAD_DIM))
    a = jnp.exp(log_a)
    mult = jnp.sqrt(-jnp.expm1(2.0 * log_a))
    bx = mult * ig * xh

    def combine(e1, e2):
        a1, b1 = e1
        a2, b2 = e2
        return a2 * a1, a2 * b1 + b2

    _, h = lax.associative_scan(combine, (a, bx), axis=1)
    return h.reshape(bsz, L, LRU_WIDTH).astype(u.dtype)


def setup_inputs(seed: int = 0) -> dict:
    key = jax.random.key(seed)
    ks = jax.random.split(key, 40)
    f32 = jnp.float32

    def nrm(k, shape, scale):
        return jax.random.normal(k, shape, f32) * scale

    def gain(k, shape):
        return 1.0 + 0.01 * jax.random.normal(k, shape, f32)

    G, N, P = S5_GROUPS, S5_STATE, S5_GROUP_CH
    lam_re = -0.5 + 0.01 * jax.random.normal(ks[3], (DEPTH, G, N), f32)
    lam_im = jnp.pi * jnp.arange(N, dtype=f32)[None, None, :] + 0.01 * jax.random.normal(ks[4], (DEPTH, G, N), f32)
    log_dt = jax.random.uniform(ks[5], (DEPTH, G), f32, math.log(1e-3), math.log(1e-1))
    u_a = jax.random.uniform(ks[16], (DEPTH, LRU_WIDTH), f32, 0.9, 0.999)
    a_base = u_a ** (1.0 / LRU_C)
    lru_lambda = jnp.log(a_base) - jnp.log1p(-a_base)

    return {
        "x": nrm(ks[0], (BATCH, SEQ, D_MODEL), 1.0),
        "p": nrm(ks[1], (DEPTH, BATCH, SEQ, PLE_DIM), 1.0),
        "g_mix": gain(ks[2], (DEPTH, D_MODEL)),
        "w_in": nrm(ks[6], (DEPTH, D_MODEL, IN_COLS), D_MODEL ** -0.5),
        "b_in": nrm(ks[7], (DEPTH, IN_COLS), 0.01),
        "lam_re": lam_re,
        "lam_im": lam_im,
        "log_dt": log_dt,
        "s5_b_re": nrm(ks[8], (DEPTH, G, N, P), (2 * P) ** -0.5),
        "s5_b_im": nrm(ks[9], (DEPTH, G, N, P), (2 * P) ** -0.5),
        "s5_c_re": nrm(ks[10], (DEPTH, G, P, N), N ** -0.5),
        "s5_c_im": nrm(ks[11], (DEPTH, G, P, N), N ** -0.5),
        "s5_d": nrm(ks[12], (DEPTH, G, P), 1.0),
        "w_glu": nrm(ks[13], (DEPTH, S5_WIDTH, S5_WIDTH), S5_WIDTH ** -0.5),
        "b_glu": nrm(ks[14], (DEPTH, S5_WIDTH), 0.01),
        "conv_w": nrm(ks[15], (DEPTH, CONV_WIDTH, LRU_WIDTH), CONV_WIDTH ** -0.5),
        "conv_b": nrm(ks[17], (DEPTH, LRU_WIDTH), 0.01),
        "w_r": nrm(ks[18], (DEPTH, LRU_HEADS, LRU_HEAD_DIM, LRU_HEAD_DIM), LRU_HEAD_DIM ** -0.5),
        "b_r": nrm(ks[19], (DEPTH, LRU_HEADS, LRU_HEAD_DIM), 0.01),
        "w_i": nrm(ks[20], (DEPTH, LRU_HEADS, LRU_HEAD_DIM, LRU_HEAD_DIM), LRU_HEAD_DIM ** -0.5),
        "b_i": nrm(ks[21], (DEPTH, LRU_HEADS, LRU_HEAD_DIM), 0.01),
        "lru_lambda": lru_lambda,
        "w_a_out": nrm(ks[22], (DEPTH, S5_WIDTH, D_MODEL), S5_WIDTH ** -0.5),
        "w_b_out": nrm(ks[23], (DEPTH, LRU_WIDTH, D_MODEL), LRU_WIDTH ** -0.5),
        "w_o": nrm(ks[24], (DEPTH, D_MODEL, D_MODEL), D_MODEL ** -0.5),
        "g_ffn": gain(ks[25], (DEPTH, D_MODEL)),
        "w_ffn_gate": nrm(ks[26], (DEPTH, D_MODEL, FFN_HIDDEN), D_MODEL ** -0.5),
        "w_ffn_up": nrm(ks[27], (DEPTH, D_MODEL, FFN_HIDDEN), D_MODEL ** -0.5),
        "w_ffn_down": nrm(ks[28], (DEPTH, FFN_HIDDEN, D_MODEL), FFN_HIDDEN ** -0.5),
        "g_ple_gate": gain(ks[29], (DEPTH, D_MODEL)),
        "w_ple_gate": nrm(ks[30], (DEPTH, D_MODEL, D_MODEL), D_MODEL ** -0.5),
        "b_ple_gate": nrm(ks[31], (DEPTH, D_MODEL), 0.01),
        "w_ple": nrm(ks[32], (DEPTH, PLE_DIM, D_MODEL), PLE_DIM ** -0.5),
        "g_ple": gain(ks[33], (DEPTH, D_MODEL)),
        "g_final": gain(ks[34], (D_MODEL,)),
    }


def reference(x, p, g_mix, w_in, b_in, lam_re, lam_im, log_dt, s5_b_re, s5_b_im,
              s5_c_re, s5_c_im, s5_d, w_glu, b_glu, conv_w, conv_b, w_r, b_r, w_i, b_i,
              lru_lambda, w_a_out, w_b_out, w_o, g_ffn, w_ffn_gate, w_ffn_up, w_ffn_down,
              g_ple_gate, w_ple_gate, b_ple_gate, w_ple, g_ple, g_final):
    s_a = S5_WIDTH
    s_b = S5_WIDTH + LRU_WIDTH
    s_g = s_b + D_MODEL
    for i in range(DEPTH):
        h = rms_norm(x, g_mix[i])
        z = h @ w_in[i] + b_in[i]
        u_a = z[..., :s_a]
        u_b = z[..., s_a:s_b]
        gate_a = jax.nn.sigmoid(z[..., s_b:s_g])
        gate_b = jax.nn.sigmoid(z[..., s_g:])
        y_a = s5_mixer(u_a, lam_re[i], lam_im[i], log_dt[i], s5_b_re[i], s5_b_im[i],
                       s5_c_re[i], s5_c_im[i], s5_d[i], w_glu[i], b_glu[i])
        y_b = rglru_mixer(u_b, conv_w[i], conv_b[i], w_r[i], b_r[i], w_i[i], b_i[i],
                          lru_lambda[i])
        merged = gate_a * (y_a @ w_a_out[i]) + gate_b * (y_b @ w_b_out[i])
        x = x + merged @ w_o[i]
        h2 = rms_norm(x, g_ffn[i])
        x = x + (jax.nn.silu(h2 @ w_ffn_gate[i]) * (h2 @ w_ffn_up[i])) @ w_ffn_down[i]
        gate_p = jax.nn.sigmoid(rms_norm(x, g_ple_gate[i]) @ w_ple_gate[i] + b_ple_gate[i])
        e = rms_norm(p[i] @ w_ple[i], g_ple[i])
        x = x + gate_p * e
    return rms_norm(x, g_final)
```

```python
import functools

import jax
import jax.numpy as jnp
from jax import lax
from jax.experimental import pallas as pl
from jax.experimental.pallas import tpu as pltpu

F32 = jnp.float32
BF16 = jnp.bfloat16

EPS = 1e-6
LRU_C = 8.0
CONV_WIDTH = 4
S5_GROUP_CH = 16
S5_STATE = 64
LRU_HEAD_DIM = 64

SUBLANES = 8
MXU_DIM = 256
TQ = 64
ROW_CHUNK = 256
FFN_ROWS = 512
FFN_HID_BLK = 256
VMEM_LIMIT = 56 * 1024 * 1024


def _rms(x, g):
    return x * lax.rsqrt(jnp.mean(x * x, axis=-1, keepdims=True) + EPS) * g


def _dot(a, b):
    return jnp.dot(a, b, preferred_element_type=F32)


def _softplus(x):
    return jnp.maximum(x, 0.0) + jnp.log1p(jnp.exp(-jnp.abs(x)))


def _mixer_kernel(x_ref, gmix_ref, win_ref, bin_ref, are_ref, aim_ref, bbd_ref,
                  cre_ref, cim_ref, dskip_ref, wglu_ref, bglu_ref, convw_ref,
                  convb_ref, wg_ref, bgr_ref, bgi_ref, lam_ref, waout_ref,
                  wbout_ref, wo_ref, o_ref,
                  xtb, hbf, ua, ubuf, xr, xi, s5st, abuf, bxbuf, lrust):
    nb, tq, d = x_ref.shape
    m = nb * tq
    s5w = ua.shape[1]
    nst = xr.shape[1]
    half_w = s5w // 2
    half_st = nst // 2
    lw = abuf.shape[1]
    hdr = (CONV_WIDTH - 1) * nb

    @pl.when(pl.program_id(0) == 0)
    def _():
        s5st[...] = jnp.zeros_like(s5st)
        lrust[...] = jnp.zeros_like(lrust)
        ubuf[pl.ds(0, hdr), :] = jnp.zeros((hdr, lw), F32)

    xtb[...] = pltpu.einshape("btd->(tb)d", x_ref[...])

    sp = LRU_C * _softplus(-lam_ref[...])

    def phase1(c, _):
        r0 = pl.multiple_of(c * ROW_CHUNK, ROW_CHUNK)
        rows = pl.ds(r0, ROW_CHUNK)
        h = _rms(xtb[rows, :], gmix_ref[...]).astype(BF16)
        hbf[rows, :] = h
        u = _dot(h, win_ref[:, 0:s5w]) + bin_ref[:, 0:s5w]
        ua[rows, :] = u
        ub16 = u.astype(BF16)
        for hh in range(2):
            pr = _dot(ub16[:, hh * half_w:(hh + 1) * half_w], bbd_ref[hh])
            xr[rows, hh * half_st:(hh + 1) * half_st] = pr[:, :half_st]
            xi[rows, hh * half_st:(hh + 1) * half_st] = pr[:, half_st:]
        ub = _dot(h, win_ref[:, s5w:s5w + lw]) + bin_ref[:, s5w:s5w + lw]
        ubuf[pl.ds(r0 + hdr, ROW_CHUNK), :] = ub
        xc = convb_ref[...] + convw_ref[CONV_WIDTH - 1:CONV_WIDTH, :] * ub
        for k in range(CONV_WIDTH - 1):
            xc = xc + convw_ref[k:k + 1, :] * ubuf[pl.ds(r0 + k * nb, ROW_CHUNK), :]
        xc16 = xc.astype(BF16)
        for j in range(lw // MXU_DIM):
            cs = slice(j * MXU_DIM, (j + 1) * MXU_DIM)
            g = _dot(xc16[:, cs], wg_ref[j])
            r = jax.nn.sigmoid(g[:, :MXU_DIM] + bgr_ref[:, cs])
            ig = jax.nn.sigmoid(g[:, MXU_DIM:] + bgi_ref[:, cs])
            log_a = -(r * sp[:, cs])
            a = jnp.exp(log_a)
            abuf[rows, cs] = a
            bxbuf[rows, cs] = (jnp.sqrt(-jnp.tanh(log_a) * (a * a + 1.0))
                               * ig * xc[:, cs])
        return 0

    lax.fori_loop(0, m // ROW_CHUNK, phase1, 0)
    ubuf[pl.ds(0, hdr), :] = ubuf[pl.ds(m, hdr), :]

    s5_lanes = 512
    for c in range(nst // s5_lanes):
        ls = slice(c * s5_lanes, (c + 1) * s5_lanes)
        ar = jnp.broadcast_to(are_ref[:, ls], (nb, s5_lanes))
        ai = jnp.broadcast_to(aim_ref[:, ls], (nb, s5_lanes))

        def s5_step(t, carry, ls=ls, ar=ar, ai=ai):
            sr, si = carry
            rows = pl.ds(pl.multiple_of(t * nb, nb), nb)
            nsr = ar * sr - ai * si + xr[rows, ls]
            nsi = ar * si + ai * sr + xi[rows, ls]
            xr[rows, ls] = nsr
            xi[rows, ls] = nsi
            return nsr, nsi

        sr, si = lax.fori_loop(0, tq, s5_step, (s5st[0, :, ls], s5st[1, :, ls]),
                               unroll=8)
        s5st[0, :, ls] = sr
        s5st[1, :, ls] = si

    def lru_step(t, hstate):
        rows = pl.ds(pl.multiple_of(t * nb, nb), nb)
        hn = abuf[rows, :] * hstate + bxbuf[rows, :]
        bxbuf[rows, :] = hn
        return hn

    lrust[...] = lax.fori_loop(0, tq, lru_step, lrust[...], unroll=8)

    def phase3(c, _):
        r0 = pl.multiple_of(c * ROW_CHUNK, ROW_CHUNK)
        rows = pl.ds(r0, ROW_CHUNK)
        ys = []
        for hh in range(2):
            ls = slice(hh * half_st, (hh + 1) * half_st)
            ys.append(_dot(xr[rows, ls].astype(BF16), cre_ref[hh])
                      - _dot(xi[rows, ls].astype(BF16), cim_ref[hh]))
        y = jnp.concatenate(ys, axis=1) + dskip_ref[...] * ua[rows, :]
        z = jax.nn.gelu(y)
        y_a = z * jax.nn.sigmoid(_dot(z.astype(BF16), wglu_ref[...]) + bglu_ref[...])
        h = hbf[rows, :]
        ga = jax.nn.sigmoid(_dot(h, win_ref[:, s5w + lw:s5w + lw + d])
                            + bin_ref[:, s5w + lw:s5w + lw + d])
        merged = ga * _dot(y_a.astype(BF16), waout_ref[...])
        gb = jax.nn.sigmoid(_dot(h, win_ref[:, s5w + lw + d:s5w + lw + 2 * d])
                            + bin_ref[:, s5w + lw + d:s5w + lw + 2 * d])
        merged = merged + gb * _dot(bxbuf[rows, :].astype(BF16), wbout_ref[...])
        xtb[rows, :] = xtb[rows, :] + _dot(merged.astype(BF16), wo_ref[...])
        return 0

    lax.fori_loop(0, m // ROW_CHUNK, phase3, 0)

    o_ref[...] = pltpu.einshape("(tb)d->btd", xtb[...], b=nb)


def _const_spec(shape):
    nd = len(shape)
    return pl.BlockSpec(shape, lambda i, _nd=nd: (0,) * _nd,
                        pipeline_mode=pl.Buffered(1))


def _mixer_call(x, consts, s5w, nst, lw):
    nb, seq, d = x.shape
    m = nb * TQ
    in_specs = [pl.BlockSpec((nb, TQ, d), lambda i: (0, i, 0))]
    in_specs += [_const_spec(c.shape) for c in consts]
    scratch = [
        pltpu.VMEM((m, d), F32),
        pltpu.VMEM((m, d), BF16),
        pltpu.VMEM((m, s5w), F32),
        pltpu.VMEM((m + (CONV_WIDTH - 1) * nb, lw), F32),
        pltpu.VMEM((m, nst), F32),
        pltpu.VMEM((m, nst), F32),
        pltpu.VMEM((2, nb, nst), F32),
        pltpu.VMEM((m, lw), F32),
        pltpu.VMEM((m, lw), F32),
        pltpu.VMEM((nb, lw), F32),
    ]
    return pl.pallas_call(
        _mixer_kernel,
        out_shape=jax.ShapeDtypeStruct(x.shape, F32),
        grid=(seq // TQ,),
        in_specs=in_specs,
        out_specs=pl.BlockSpec((nb, TQ, d), lambda i: (0, i, 0)),
        scratch_shapes=scratch,
        compiler_params=pltpu.CompilerParams(
            dimension_semantics=("arbitrary",), vmem_limit_bytes=VMEM_LIMIT),
        name="mixer",
    )(x, *consts)


def _ffn_kernel(x_ref, p_ref, gffn_ref, wgate_ref, wup_ref, wdown_ref, gpg_ref,
                wpg_ref, bpg_ref, wple_ref, gple_ref, gfin_ref, o_ref):
    x = x_ref[...]
    h2 = _rms(x, gffn_ref[...]).astype(BF16)
    hid = wgate_ref.shape[1]
    acc = x
    for j in range(hid // FFN_HID_BLK):
        cs = slice(j * FFN_HID_BLK, (j + 1) * FFN_HID_BLK)
        a = jax.nn.silu(_dot(h2, wgate_ref[:, cs])) * _dot(h2, wup_ref[:, cs])
        acc = acc + _dot(a.astype(BF16), wdown_ref[cs, :])
    x2 = acc
    gate_p = jax.nn.sigmoid(
        _dot(_rms(x2, gpg_ref[...]).astype(BF16), wpg_ref[...]) + bpg_ref[...])
    e = _rms(_dot(p_ref[...].astype(BF16), wple_ref[...]), gple_ref[...])
    x3 = x2 + gate_p * e
    o_ref[...] = _rms(x3, gfin_ref[...])


def _ffn_call(x1, p2, consts):
    t, d = x1.shape
    pd = p2.shape[1]
    in_specs = [pl.BlockSpec((FFN_ROWS, d), lambda i: (i, 0)),
                pl.BlockSpec((FFN_ROWS, pd), lambda i: (i, 0))]
    in_specs += [_const_spec(c.shape) for c in consts]
    return pl.pallas_call(
        _ffn_kernel,
        out_shape=jax.ShapeDtypeStruct((t, d), F32),
        grid=(t // FFN_ROWS,),
        in_specs=in_specs,
        out_specs=pl.BlockSpec((FFN_ROWS, d), lambda i: (i, 0)),
        compiler_params=pltpu.CompilerParams(
            dimension_semantics=("arbitrary",), vmem_limit_bytes=VMEM_LIMIT),
        name="ffn",
    )(x1, p2, *consts)


def _block_diag(blocks):
    n, r, c = blocks.shape
    eye = jnp.eye(n, dtype=blocks.dtype)
    return jnp.einsum('grc,gh->grhc', blocks, eye).reshape(n * r, n * c)


def _s5_params(lam_re, lam_im, log_dt, b_re, b_im, c_re, c_im):
    g, n = lam_re.shape
    dt = jnp.exp(log_dt)[:, None]
    mag = jnp.exp(lam_re * dt)
    ar = mag * jnp.cos(lam_im * dt)
    ai = mag * jnp.sin(lam_im * dt)
    den = lam_re * lam_re + lam_im * lam_im
    nr = ar - 1.0
    fr = (nr * lam_re + ai * lam_im) / den
    fi = (ai * lam_re - nr * lam_im) / den
    bbr = fr[..., None] * b_re - fi[..., None] * b_im
    bbi = fr[..., None] * b_im + fi[..., None] * b_re
    hg = g // 2
    bbd, cre, cim = [], [], []
    for h in range(2):
        gs = slice(h * hg, (h + 1) * hg)
        re_bd = _block_diag(jnp.swapaxes(bbr[gs], 1, 2))
        im_bd = _block_diag(jnp.swapaxes(bbi[gs], 1, 2))
        bbd.append(jnp.concatenate([re_bd, im_bd], axis=1))
        cre.append(_block_diag(jnp.swapaxes(c_re[gs], 1, 2)))
        cim.append(_block_diag(jnp.swapaxes(c_im[gs], 1, 2)))
    return (ar.reshape(1, g * n), ai.reshape(1, g * n),
            jnp.stack(bbd).astype(BF16), jnp.stack(cre).astype(BF16),
            jnp.stack(cim).astype(BF16))


def _lru_gate_weights(w_r, w_i):
    heads = w_r.shape[0]
    per = MXU_DIM // LRU_HEAD_DIM
    tiles = []
    for j in range(heads // per):
        hs = slice(j * per, (j + 1) * per)
        tiles.append(jnp.concatenate([_block_diag(w_r[hs]), _block_diag(w_i[hs])],
                                     axis=1))
    return jnp.stack(tiles).astype(BF16)


def kernel(x, p, g_mix, w_in, b_in, lam_re, lam_im, log_dt, s5_b_re, s5_b_im, s5_c_re, s5_c_im, s5_d, w_glu, b_glu, conv_w, conv_b, w_r, b_r, w_i, b_i, lru_lambda, w_a_out, w_b_out, w_o, g_ffn, w_ffn_gate, w_ffn_up, w_ffn_down, g_ple_gate, w_ple_gate, b_ple_gate, w_ple, g_ple, g_final):
    bsz, seq, d = x.shape
    assert w_in.shape[0] == 1
    assert bsz == SUBLANES and seq % TQ == 0 and (bsz * TQ) % ROW_CHUNK == 0
    assert (bsz * seq) % FFN_ROWS == 0

    def row(v):
        return v.reshape(1, -1).astype(F32)

    are, aim, bbd, cre, cim = _s5_params(
        lam_re[0], lam_im[0], log_dt[0], s5_b_re[0], s5_b_im[0],
        s5_c_re[0], s5_c_im[0])
    mixer_consts = (
        row(g_mix[0]), w_in[0].astype(BF16), row(b_in[0]), are, aim, bbd,
        cre, cim, row(s5_d[0]), w_glu[0].astype(BF16), row(b_glu[0]),
        conv_w[0].astype(F32), row(conv_b[0]),
        _lru_gate_weights(w_r[0], w_i[0]), row(b_r[0]), row(b_i[0]),
        row(lru_lambda[0]), w_a_out[0].astype(BF16), w_b_out[0].astype(BF16),
        w_o[0].astype(BF16))
    x1 = _mixer_call(x, mixer_consts, s5w=w_glu.shape[1], nst=are.shape[1],
                     lw=conv_w.shape[2])
    ffn_consts = (
        row(g_ffn[0]), w_ffn_gate[0].astype(BF16), w_ffn_up[0].astype(BF16),
        w_ffn_down[0].astype(BF16), row(g_ple_gate[0]),
        w_ple_gate[0].astype(BF16), row(b_ple_gate[0]),
        w_ple[0].astype(BF16), row(g_ple[0]), row(g_final))
    out = _ffn_call(x1.reshape(bsz * seq, d), p[0].reshape(bsz * seq, -1),
                    ffn_consts)
    return out.reshape(bsz, seq, d)
```

```python
import functools

import jax
import jax.numpy as jnp
from jax import lax
from jax.experimental import pallas as pl
from jax.experimental.pallas import tpu as pltpu

F32 = jnp.float32
BF16 = jnp.bfloat16

EPS = 1e-6
LRU_C = 8.0
CONV_WIDTH = 4
S5_GROUP_CH = 16
S5_STATE = 64
LRU_HEAD_DIM = 64

SUBLANES = 8
MXU_DIM = 256
TQ = 64
ROW_CHUNK = 512
FFN_ROWS = 512
FFN_HID_BLK = 256
VMEM_LIMIT = 56 * 1024 * 1024


def _rms(x, g):
    return x * lax.rsqrt(jnp.mean(x * x, axis=-1, keepdims=True) + EPS) * g


def _dot(a, b):
    return jnp.dot(a, b, preferred_element_type=F32)


def _softplus(x):
    return jnp.maximum(x, 0.0) + jnp.log1p(jnp.exp(-jnp.abs(x)))


def _mixer_kernel(x_ref, gmix_ref, win_ref, bin_ref, are_ref, aim_ref, bbd_ref,
                  cre_ref, cim_ref, dskip_ref, wglu_ref, bglu_ref, convw_ref,
                  convb_ref, wg_ref, bgr_ref, bgi_ref, lam_ref, waout_ref,
                  wbout_ref, wo_ref, o_ref,
                  xtb, gates, ua, ubuf, xr, xi, s5st, abuf, bxbuf, lrust):
    nb, tq, d = x_ref.shape
    m = nb * tq
    s5w = ua.shape[1]
    nst = xr.shape[1]
    half_w = s5w // 2
    half_st = nst // 2
    lw = abuf.shape[1]
    hdr = (CONV_WIDTH - 1) * nb

    @pl.when(pl.program_id(0) == 0)
    def _():
        s5st[...] = jnp.zeros_like(s5st)
        lrust[...] = jnp.zeros_like(lrust)
        ubuf[pl.ds(0, hdr), :] = jnp.zeros((hdr, lw), F32)

    xtb[...] = jnp.swapaxes(x_ref[...], 0, 1).reshape(m, d)

    sp = LRU_C * _softplus(-lam_ref[...])

    def phase1(c, _):
        r0 = pl.multiple_of(c * ROW_CHUNK, ROW_CHUNK)
        rows = pl.ds(r0, ROW_CHUNK)
        h = _rms(xtb[rows, :], gmix_ref[...]).astype(BF16)
        g0 = s5w + lw
        n_blk = lw // MXU_DIM
        gate_w = 2 * d // n_blk

        def in_proj(lo, width):
            return _dot(h, win_ref[:, lo:lo + width]) + bin_ref[:, lo:lo + width]

        u = in_proj(0, s5w)
        ua[rows, :] = u
        ub16 = u.astype(BF16)
        for j in range(n_blk):
            cs = slice(j * MXU_DIM, (j + 1) * MXU_DIM)
            ub = in_proj(s5w + j * MXU_DIM, MXU_DIM)
            ubuf[pl.ds(r0 + hdr, ROW_CHUNK), cs] = ub
            xc = convb_ref[:, cs] + convw_ref[CONV_WIDTH - 1:CONV_WIDTH, cs] * ub
            for k in range(CONV_WIDTH - 1):
                xc = xc + (convw_ref[k:k + 1, cs]
                           * ubuf[pl.ds(r0 + k * nb, ROW_CHUNK), cs])
            if j < 2:
                pr = _dot(ub16[:, j * half_w:(j + 1) * half_w], bbd_ref[j])
                xr[rows, j * half_st:(j + 1) * half_st] = pr[:, :half_st]
                xi[rows, j * half_st:(j + 1) * half_st] = pr[:, half_st:]
            g = _dot(xc.astype(BF16), wg_ref[j])
            r = jax.nn.sigmoid(g[:, :MXU_DIM] + bgr_ref[:, cs])
            ig = jax.nn.sigmoid(g[:, MXU_DIM:] + bgi_ref[:, cs])
            log_a = -(r * sp[:, cs])
            a = jnp.exp(log_a)
            abuf[rows, cs] = a
            bxbuf[rows, cs] = (jnp.sqrt(-jnp.tanh(log_a) * (a * a + 1.0))
                               * ig * xc)
            gates[rows, j * gate_w:(j + 1) * gate_w] = jax.nn.sigmoid(
                in_proj(g0 + j * gate_w, gate_w)).astype(BF16)
        return 0

    lax.fori_loop(0, m // ROW_CHUNK, phase1, 0)
    ubuf[pl.ds(0, hdr), :] = ubuf[pl.ds(m, hdr), :]

    s5_lanes = 512
    for c in range(nst // s5_lanes):
        ls = slice(c * s5_lanes, (c + 1) * s5_lanes)
        ar = jnp.broadcast_to(are_ref[:, ls], (nb, s5_lanes))
        ai = jnp.broadcast_to(aim_ref[:, ls], (nb, s5_lanes))

        def s5_step(t, carry, ls=ls, ar=ar, ai=ai):
            sr, si = carry
            rows = pl.ds(pl.multiple_of(t * nb, nb), nb)
            nsr = ar * sr - ai * si + xr[rows, ls]
            nsi = ar * si + ai * sr + xi[rows, ls]
            xr[rows, ls] = nsr
            xi[rows, ls] = nsi
            return nsr, nsi

        sr, si = lax.fori_loop(0, tq, s5_step, (s5st[0, :, ls], s5st[1, :, ls]),
                               unroll=8)
        s5st[0, :, ls] = sr
        s5st[1, :, ls] = si

    def lru_step(t, hstate):
        rows = pl.ds(pl.multiple_of(t * nb, nb), nb)
        hn = abuf[rows, :] * hstate + bxbuf[rows, :]
        bxbuf[rows, :] = hn
        return hn

    lrust[...] = lax.fori_loop(0, tq, lru_step, lrust[...], unroll=8)

    def phase3(c, _):
        r0 = pl.multiple_of(c * ROW_CHUNK, ROW_CHUNK)
        rows = pl.ds(r0, ROW_CHUNK)
        ys = []
        for hh in range(2):
            ls = slice(hh * half_st, (hh + 1) * half_st)
            ys.append(_dot(xr[rows, ls].astype(BF16), cre_ref[hh])
                      - _dot(xi[rows, ls].astype(BF16), cim_ref[hh]))
        y = jnp.concatenate(ys, axis=1) + dskip_ref[...] * ua[rows, :]
        z = jax.nn.gelu(y)
        y_a = z * jax.nn.sigmoid(_dot(z.astype(BF16), wglu_ref[...]) + bglu_ref[...])
        merged = gates[rows, 0:d] * _dot(y_a.astype(BF16), waout_ref[...])
        merged = merged + gates[rows, d:2 * d] * _dot(
            bxbuf[rows, :].astype(BF16), wbout_ref[...])
        xtb[rows, :] = xtb[rows, :] + _dot(merged.astype(BF16), wo_ref[...])
        return 0

    lax.fori_loop(0, m // ROW_CHUNK, phase3, 0)

    o_ref[...] = jnp.swapaxes(xtb[...].reshape(tq, nb, d), 0, 1)


def _const_spec(shape):
    nd = len(shape)
    return pl.BlockSpec(shape, lambda i, _nd=nd: (0,) * _nd,
                        pipeline_mode=pl.Buffered(1))


def _mixer_call(x, consts, s5w, nst, lw):
    nb, seq, d = x.shape
    m = nb * TQ
    in_specs = [pl.BlockSpec((nb, TQ, d), lambda i: (0, i, 0))]
    in_specs += [_const_spec(c.shape) for c in consts]
    scratch = [
        pltpu.VMEM((m, d), F32),
        pltpu.VMEM((m, 2 * d), BF16),
        pltpu.VMEM((m, s5w), F32),
        pltpu.VMEM((m + (CONV_WIDTH - 1) * nb, lw), F32),
        pltpu.VMEM((m, nst), F32),
        pltpu.VMEM((m, nst), F32),
        pltpu.VMEM((2, nb, nst), F32),
        pltpu.VMEM((m, lw), F32),
        pltpu.VMEM((m, lw), F32),
        pltpu.VMEM((nb, lw), F32),
    ]
    return pl.pallas_call(
        _mixer_kernel,
        out_shape=jax.ShapeDtypeStruct(x.shape, F32),
        grid=(seq // TQ,),
        in_specs=in_specs,
        out_specs=pl.BlockSpec((nb, TQ, d), lambda i: (0, i, 0)),
        scratch_shapes=scratch,
        compiler_params=pltpu.CompilerParams(
            dimension_semantics=("arbitrary",), vmem_limit_bytes=VMEM_LIMIT),
        name="mixer",
    )(x, *consts)


def _ffn_kernel(x_ref, p_ref, gffn_ref, wgate_ref, wup_ref, wdown_ref, gpg_ref,
                wpg_ref, bpg_ref, wple_ref, gple_ref, gfin_ref, o_ref):
    x = x_ref[...]
    h2 = _rms(x, gffn_ref[...]).astype(BF16)
    hid = wgate_ref.shape[1]
    acc = x
    for j in range(hid // FFN_HID_BLK):
        cs = slice(j * FFN_HID_BLK, (j + 1) * FFN_HID_BLK)
        a = jax.nn.silu(_dot(h2, wgate_ref[:, cs])) * _dot(h2, wup_ref[:, cs])
        acc = acc + _dot(a.astype(BF16), wdown_ref[cs, :])
    x2 = acc
    gate_p = jax.nn.sigmoid(
        _dot(_rms(x2, gpg_ref[...]).astype(BF16), wpg_ref[...]) + bpg_ref[...])
    e = _rms(_dot(p_ref[...].astype(BF16), wple_ref[...]), gple_ref[...])
    x3 = x2 + gate_p * e
    o_ref[...] = _rms(x3, gfin_ref[...])


def _ffn_call(x1, p2, consts):
    t, d = x1.shape
    pd = p2.shape[1]
    in_specs = [pl.BlockSpec((FFN_ROWS, d), lambda i: (i, 0)),
                pl.BlockSpec((FFN_ROWS, pd), lambda i: (i, 0))]
    in_specs += [_const_spec(c.shape) for c in consts]
    return pl.pallas_call(
        _ffn_kernel,
        out_shape=jax.ShapeDtypeStruct((t, d), F32),
        grid=(t // FFN_ROWS,),
        in_specs=in_specs,
        out_specs=pl.BlockSpec((FFN_ROWS, d), lambda i: (i, 0)),
        compiler_params=pltpu.CompilerParams(
            dimension_semantics=("arbitrary",), vmem_limit_bytes=VMEM_LIMIT),
        name="ffn",
    )(x1, p2, *consts)


def _block_diag(blocks):
    n, r, c = blocks.shape
    eye = jnp.eye(n, dtype=blocks.dtype)
    return jnp.einsum('grc,gh->grhc', blocks, eye).reshape(n * r, n * c)


def _s5_params(lam_re, lam_im, log_dt, b_re, b_im, c_re, c_im):
    g, n = lam_re.shape
    dt = jnp.exp(log_dt)[:, None]
    mag = jnp.exp(lam_re * dt)
    ar = mag * jnp.cos(lam_im * dt)
    ai = mag * jnp.sin(lam_im * dt)
    den = lam_re * lam_re + lam_im * lam_im
    nr = ar - 1.0
    fr = (nr * lam_re + ai * lam_im) / den
    fi = (ai * lam_re - nr * lam_im) / den
    bbr = fr[..., None] * b_re - fi[..., None] * b_im
    bbi = fr[..., None] * b_im + fi[..., None] * b_re
    hg = g // 2
    bbd, cre, cim = [], [], []
    for h in range(2):
        gs = slice(h * hg, (h + 1) * hg)
        re_bd = _block_diag(jnp.swapaxes(bbr[gs], 1, 2))
        im_bd = _block_diag(jnp.swapaxes(bbi[gs], 1, 2))
        bbd.append(jnp.concatenate([re_bd, im_bd], axis=1))
        cre.append(_block_diag(jnp.swapaxes(c_re[gs], 1, 2)))
        cim.append(_block_diag(jnp.swapaxes(c_im[gs], 1, 2)))
    return (ar.reshape(1, g * n), ai.reshape(1, g * n),
            jnp.stack(bbd).astype(BF16), jnp.stack(cre).astype(BF16),
            jnp.stack(cim).astype(BF16))


def _lru_gate_weights(w_r, w_i):
    heads = w_r.shape[0]
    per = MXU_DIM // LRU_HEAD_DIM
    tiles = []
    for j in range(heads // per):
        hs = slice(j * per, (j + 1) * per)
        tiles.append(jnp.concatenate([_block_diag(w_r[hs]), _block_diag(w_i[hs])],
                                     axis=1))
    return jnp.stack(tiles).astype(BF16)


def kernel(x, p, g_mix, w_in, b_in, lam_re, lam_im, log_dt, s5_b_re, s5_b_im, s5_c_re, s5_c_im, s5_d, w_glu, b_glu, conv_w, conv_b, w_r, b_r, w_i, b_i, lru_lambda, w_a_out, w_b_out, w_o, g_ffn, w_ffn_gate, w_ffn_up, w_ffn_down, g_ple_gate, w_ple_gate, b_ple_gate, w_ple, g_ple, g_final):
    bsz, seq, d = x.shape
    assert w_in.shape[0] == 1
    assert bsz == SUBLANES and seq % TQ == 0 and (bsz * TQ) % ROW_CHUNK == 0
    assert (bsz * seq) % FFN_ROWS == 0

    def row(v):
        return v.reshape(1, -1).astype(F32)

    are, aim, bbd, cre, cim = _s5_params(
        lam_re[0], lam_im[0], log_dt[0], s5_b_re[0], s5_b_im[0],
        s5_c_re[0], s5_c_im[0])
    mixer_consts = (
        row(g_mix[0]), w_in[0].astype(BF16), row(b_in[0]), are, aim, bbd,
        cre, cim, row(s5_d[0]), w_glu[0].astype(BF16), row(b_glu[0]),
        conv_w[0].astype(F32), row(conv_b[0]),
        _lru_gate_weights(w_r[0], w_i[0]), row(b_r[0]), row(b_i[0]),
        row(lru_lambda[0]), w_a_out[0].astype(BF16), w_b_out[0].astype(BF16),
        w_o[0].astype(BF16))
    x1 = _mixer_call(x, mixer_consts, s5w=w_glu.shape[1], nst=are.shape[1],
                     lw=conv_w.shape[2])
    ffn_consts = (
        row(g_ffn[0]), w_ffn_gate[0].astype(BF16), w_ffn_up[0].astype(BF16),
        w_ffn_down[0].astype(BF16), row(g_ple_gate[0]),
        w_ple_gate[0].astype(BF16), row(b_ple_gate[0]),
        w_ple[0].astype(BF16), row(g_ple[0]), row(g_final))
    out = _ffn_call(x1.reshape(bsz * seq, d), p[0].reshape(bsz * seq, -1),
                    ffn_consts)
    return out.reshape(bsz, seq, d)
```

```python
import functools

import jax
import jax.numpy as jnp
from jax import lax
from jax.experimental import pallas as pl
from jax.experimental.pallas import tpu as pltpu

F32 = jnp.float32
BF16 = jnp.bfloat16

EPS = 1e-6
LRU_C = 8.0
CONV_WIDTH = 4
S5_GROUP_CH = 16
S5_STATE = 64
LRU_HEAD_DIM = 64

SUBLANES = 8
MXU_DIM = 256
TQ = 64
ROW_CHUNK = 512
SCAN_LANES = 512
SCAN_UNROLL = 8
FFN_ROWS = 1024
FFN_SUB_ROWS = 512
FFN_HID_BLK = 256
VMEM_LIMIT = 56 * 1024 * 1024


def _rms(x, g):
    return x * lax.rsqrt(jnp.mean(x * x, axis=-1, keepdims=True) + EPS) * g


def _dot(a, b):
    return jnp.dot(a, b, preferred_element_type=F32)


def _sigmoid(x):
    return 0.5 * jnp.tanh(0.5 * x) + 0.5


def _sqrt_nonneg(v):
    return jnp.where(v > 0.0, v * lax.rsqrt(v), 0.0)


def _softplus(x):
    return jnp.maximum(x, 0.0) + jnp.log1p(jnp.exp(-jnp.abs(x)))


def _mixer_kernel(x_ref, gmix_ref, win_ref, bin_ref, are_ref, aim_ref, bbd_ref,
                  cre_ref, cim_ref, dskip_ref, wglu_ref, bglu_ref, convw_ref,
                  convb_ref, wg_ref, bgr_ref, bgi_ref, lam_ref, waout_ref,
                  wbout_ref, wo_ref, o_ref,
                  xtb, gates, ua, ubuf, xr, xi, s5st, abuf, bxbuf, lrust):
    nb, tq, d = x_ref.shape
    m = nb * tq
    s5w = ua.shape[1]
    nst = xr.shape[1]
    half_w = s5w // 2
    half_st = nst // 2
    lw = abuf.shape[1]
    hdr = (CONV_WIDTH - 1) * nb

    @pl.when(pl.program_id(0) == 0)
    def _():
        s5st[...] = jnp.zeros_like(s5st)
        lrust[...] = jnp.zeros_like(lrust)
        ubuf[pl.ds(0, hdr), :] = jnp.zeros((hdr, lw), F32)

    xtb[...] = jnp.swapaxes(x_ref[...], 0, 1).reshape(m, d)

    sp = LRU_C * _softplus(-lam_ref[...])

    def phase1(c, _):
        r0 = pl.multiple_of(c * ROW_CHUNK, ROW_CHUNK)
        rows = pl.ds(r0, ROW_CHUNK)
        h = _rms(xtb[rows, :], gmix_ref[...]).astype(BF16)
        g0 = s5w + lw
        n_blk = lw // MXU_DIM
        gate_w = 2 * d // n_blk

        def in_proj(lo, width):
            return _dot(h, win_ref[:, lo:lo + width]) + bin_ref[:, lo:lo + width]

        u = in_proj(0, s5w)
        ua[rows, :] = u
        ub16 = u.astype(BF16)
        for j in range(n_blk):
            cs = slice(j * MXU_DIM, (j + 1) * MXU_DIM)
            ub = in_proj(s5w + j * MXU_DIM, MXU_DIM)
            ubuf[pl.ds(r0 + hdr, ROW_CHUNK), cs] = ub
            xc = convb_ref[:, cs] + convw_ref[CONV_WIDTH - 1:CONV_WIDTH, cs] * ub
            for k in range(CONV_WIDTH - 1):
                xc = xc + (convw_ref[k:k + 1, cs]
                           * ubuf[pl.ds(r0 + k * nb, ROW_CHUNK), cs])
            if j < 2:
                pr = _dot(ub16[:, j * half_w:(j + 1) * half_w], bbd_ref[j])
                xr[rows, j * half_st:(j + 1) * half_st] = pr[:, :half_st]
                xi[rows, j * half_st:(j + 1) * half_st] = pr[:, half_st:]
            g = _dot(xc.astype(BF16), wg_ref[j])
            r = _sigmoid(g[:, :MXU_DIM] + bgr_ref[:, cs])
            ig = _sigmoid(g[:, MXU_DIM:] + bgi_ref[:, cs])
            a = jnp.exp(-(r * sp[:, cs]))
            abuf[rows, cs] = a
            bxbuf[rows, cs] = _sqrt_nonneg(1.0 - a * a) * ig * xc
            gates[rows, j * gate_w:(j + 1) * gate_w] = _sigmoid(
                in_proj(g0 + j * gate_w, gate_w)).astype(BF16)
        return 0

    lax.fori_loop(0, m // ROW_CHUNK, phase1, 0)
    ubuf[pl.ds(0, hdr), :] = ubuf[pl.ds(m, hdr), :]

    n_pass = nst // SCAN_LANES
    lru_lanes = lw // n_pass
    for c in range(n_pass):
        ls = slice(c * SCAN_LANES, (c + 1) * SCAN_LANES)
        ll = slice(c * lru_lanes, (c + 1) * lru_lanes)
        ar = jnp.broadcast_to(are_ref[:, ls], (nb, SCAN_LANES))
        ai = jnp.broadcast_to(aim_ref[:, ls], (nb, SCAN_LANES))

        def scan_step(t, carry, ls=ls, ll=ll, ar=ar, ai=ai):
            sr, si, hl = carry
            rows = pl.ds(pl.multiple_of(t * nb, nb), nb)
            nsr = ar * sr - ai * si + xr[rows, ls]
            nsi = ar * si + ai * sr + xi[rows, ls]
            xr[rows, ls] = nsr
            xi[rows, ls] = nsi
            hn = abuf[rows, ll] * hl + bxbuf[rows, ll]
            bxbuf[rows, ll] = hn
            return nsr, nsi, hn

        sr, si, hl = lax.fori_loop(
            0, tq, scan_step, (s5st[0, :, ls], s5st[1, :, ls], lrust[:, ll]),
            unroll=SCAN_UNROLL)
        s5st[0, :, ls] = sr
        s5st[1, :, ls] = si
        lrust[:, ll] = hl

    def phase3(c, _):
        r0 = pl.multiple_of(c * ROW_CHUNK, ROW_CHUNK)
        rows = pl.ds(r0, ROW_CHUNK)
        ys = []
        for hh in range(2):
            ls = slice(hh * half_st, (hh + 1) * half_st)
            ys.append(_dot(xr[rows, ls].astype(BF16), cre_ref[hh])
                      - _dot(xi[rows, ls].astype(BF16), cim_ref[hh]))
        y = jnp.concatenate(ys, axis=1) + dskip_ref[...] * ua[rows, :]
        z = jax.nn.gelu(y)
        y_a = z * _sigmoid(_dot(z.astype(BF16), wglu_ref[...]) + bglu_ref[...])
        merged = gates[rows, 0:d] * _dot(y_a.astype(BF16), waout_ref[...])
        merged = merged + gates[rows, d:2 * d] * _dot(
            bxbuf[rows, :].astype(BF16), wbout_ref[...])
        xtb[rows, :] = xtb[rows, :] + _dot(merged.astype(BF16), wo_ref[...])
        return 0

    lax.fori_loop(0, m // ROW_CHUNK, phase3, 0)

    o_ref[...] = jnp.swapaxes(xtb[...].reshape(tq, nb, d), 0, 1)


def _const_spec(shape):
    nd = len(shape)
    return pl.BlockSpec(shape, lambda i, _nd=nd: (0,) * _nd,
                        pipeline_mode=pl.Buffered(1))


def _mixer_call(x, consts, s5w, nst, lw):
    nb, seq, d = x.shape
    m = nb * TQ
    in_specs = [pl.BlockSpec((nb, TQ, d), lambda i: (0, i, 0))]
    in_specs += [_const_spec(c.shape) for c in consts]
    scratch = [
        pltpu.VMEM((m, d), F32),
        pltpu.VMEM((m, 2 * d), BF16),
        pltpu.VMEM((m, s5w), F32),
        pltpu.VMEM((m + (CONV_WIDTH - 1) * nb, lw), F32),
        pltpu.VMEM((m, nst), F32),
        pltpu.VMEM((m, nst), F32),
        pltpu.VMEM((2, nb, nst), F32),
        pltpu.VMEM((m, lw), F32),
        pltpu.VMEM((m, lw), F32),
        pltpu.VMEM((nb, lw), F32),
    ]
    return pl.pallas_call(
        _mixer_kernel,
        out_shape=jax.ShapeDtypeStruct(x.shape, F32),
        grid=(seq // TQ,),
        in_specs=in_specs,
        out_specs=pl.BlockSpec((nb, TQ, d), lambda i: (0, i, 0)),
        scratch_shapes=scratch,
        compiler_params=pltpu.CompilerParams(
            dimension_semantics=("arbitrary",), vmem_limit_bytes=VMEM_LIMIT),
        name="mixer",
    )(x, *consts)


def _ffn_kernel(x_ref, p_ref, gffn_ref, wgate_ref, wup_ref, wdown_ref, gpg_ref,
                wpg_ref, bpg_ref, wple_ref, gple_ref, gfin_ref, o_ref):
    hid = wgate_ref.shape[1]
    for s in range(FFN_ROWS // FFN_SUB_ROWS):
        rows = pl.ds(s * FFN_SUB_ROWS, FFN_SUB_ROWS)
        x = x_ref[rows, :]
        h2 = _rms(x, gffn_ref[...]).astype(BF16)
        acc = x
        for j in range(hid // FFN_HID_BLK):
            cs = slice(j * FFN_HID_BLK, (j + 1) * FFN_HID_BLK)
            gt = _dot(h2, wgate_ref[:, cs])
            a = gt * _sigmoid(gt) * _dot(h2, wup_ref[:, cs])
            acc = acc + _dot(a.astype(BF16), wdown_ref[cs, :])
        x2 = acc
        gate_p = _sigmoid(
            _dot(_rms(x2, gpg_ref[...]).astype(BF16), wpg_ref[...]) + bpg_ref[...])
        e = _rms(_dot(p_ref[rows, :].astype(BF16), wple_ref[...]), gple_ref[...])
        x3 = x2 + gate_p * e
        o_ref[rows, :] = _rms(x3, gfin_ref[...])


def _ffn_call(x1, p, consts):
    bsz, seq, d = x1.shape
    pd = p.shape[-1]
    per_seq = seq // FFN_ROWS
    in_specs = [
        pl.BlockSpec((None, FFN_ROWS, d), lambda i: (i // per_seq, i % per_seq, 0)),
        pl.BlockSpec((None, None, FFN_ROWS, pd),
                     lambda i: (0, i // per_seq, i % per_seq, 0))]
    in_specs += [_const_spec(c.shape) for c in consts]
    return pl.pallas_call(
        _ffn_kernel,
        out_shape=jax.ShapeDtypeStruct((bsz, seq, d), F32),
        grid=(bsz * per_seq,),
        in_specs=in_specs,
        out_specs=pl.BlockSpec((None, FFN_ROWS, d),
                               lambda i: (i // per_seq, i % per_seq, 0)),
        compiler_params=pltpu.CompilerParams(
            dimension_semantics=("arbitrary",), vmem_limit_bytes=VMEM_LIMIT),
        name="ffn",
    )(x1, p, *consts)


def _block_diag(blocks):
    n, r, c = blocks.shape
    eye = jnp.eye(n, dtype=blocks.dtype)
    return jnp.einsum('grc,gh->grhc', blocks, eye).reshape(n * r, n * c)


def _s5_params(lam_re, lam_im, log_dt, b_re, b_im, c_re, c_im):
    g, n = lam_re.shape
    dt = jnp.exp(log_dt)[:, None]
    mag = jnp.exp(lam_re * dt)
    ar = mag * jnp.cos(lam_im * dt)
    ai = mag * jnp.sin(lam_im * dt)
    den = lam_re * lam_re + lam_im * lam_im
    nr = ar - 1.0
    fr = (nr * lam_re + ai * lam_im) / den
    fi = (ai * lam_re - nr * lam_im) / den
    bbr = fr[..., None] * b_re - fi[..., None] * b_im
    bbi = fr[..., None] * b_im + fi[..., None] * b_re
    hg = g // 2
    bbd, cre, cim = [], [], []
    for h in range(2):
        gs = slice(h * hg, (h + 1) * hg)
        re_bd = _block_diag(jnp.swapaxes(bbr[gs], 1, 2))
        im_bd = _block_diag(jnp.swapaxes(bbi[gs], 1, 2))
        bbd.append(jnp.concatenate([re_bd, im_bd], axis=1))
        cre.append(_block_diag(jnp.swapaxes(c_re[gs], 1, 2)))
        cim.append(_block_diag(jnp.swapaxes(c_im[gs], 1, 2)))
    return (ar.reshape(1, g * n), ai.reshape(1, g * n),
            jnp.stack(bbd).astype(BF16), jnp.stack(cre).astype(BF16),
            jnp.stack(cim).astype(BF16))


def _lru_gate_weights(w_r, w_i):
    heads = w_r.shape[0]
    per = MXU_DIM // LRU_HEAD_DIM
    tiles = []
    for j in range(heads // per):
        hs = slice(j * per, (j + 1) * per)
        tiles.append(jnp.concatenate([_block_diag(w_r[hs]), _block_diag(w_i[hs])],
                                     axis=1))
    return jnp.stack(tiles).astype(BF16)


def kernel(x, p, g_mix, w_in, b_in, lam_re, lam_im, log_dt, s5_b_re, s5_b_im, s5_c_re, s5_c_im, s5_d, w_glu, b_glu, conv_w, conv_b, w_r, b_r, w_i, b_i, lru_lambda, w_a_out, w_b_out, w_o, g_ffn, w_ffn_gate, w_ffn_up, w_ffn_down, g_ple_gate, w_ple_gate, b_ple_gate, w_ple, g_ple, g_final):
    bsz, seq, d = x.shape
    assert w_in.shape[0] == 1
    assert bsz == SUBLANES and seq % TQ == 0 and (bsz * TQ) % ROW_CHUNK == 0
    assert (bsz * seq) % FFN_ROWS == 0

    def row(v):
        return v.reshape(1, -1).astype(F32)

    are, aim, bbd, cre, cim = _s5_params(
        lam_re[0], lam_im[0], log_dt[0], s5_b_re[0], s5_b_im[0],
        s5_c_re[0], s5_c_im[0])
    mixer_consts = (
        row(g_mix[0]), w_in[0].astype(BF16), row(b_in[0]), are, aim, bbd,
        cre, cim, row(s5_d[0]), w_glu[0].astype(BF16), row(b_glu[0]),
        conv_w[0].astype(F32), row(conv_b[0]),
        _lru_gate_weights(w_r[0], w_i[0]), row(b_r[0]), row(b_i[0]),
        row(lru_lambda[0]), w_a_out[0].astype(BF16), w_b_out[0].astype(BF16),
        w_o[0].astype(BF16))
    x1 = _mixer_call(x, mixer_consts, s5w=w_glu.shape[1], nst=are.shape[1],
                     lw=conv_w.shape[2])
    ffn_consts = (
        row(g_ffn[0]), w_ffn_gate[0].astype(BF16), w_ffn_up[0].astype(BF16),
        w_ffn_down[0].astype(BF16), row(g_ple_gate[0]),
        w_ple_gate[0].astype(BF16), row(b_ple_gate[0]),
        w_ple[0].astype(BF16), row(g_ple[0]), row(g_final))
    return _ffn_call(x1, p, ffn_consts)
```

```python
import functools

import jax
import jax.numpy as jnp
from jax import lax
from jax.experimental import pallas as pl
from jax.experimental.pallas import tpu as pltpu

F32 = jnp.float32
BF16 = jnp.bfloat16

EPS = 1e-6
LRU_C = 8.0
CONV_WIDTH = 4
LRU_HEAD_DIM = 64

SUBLANES = 8
LANES = 128
MXU_DIM = 256
TQ = 64
SCAN_LANES = 512
SCAN_UNROLL = 8
MIXER_STAGE_ROWS = 128
FFN_ROWS = 1024
FFN_SUB_ROWS = 512
FFN_HID_BLK = 256
FFN_STAGE_ROWS = 256
VMEM_LIMIT = 56 * 1024 * 1024


def _rms(x, g):
    return x * lax.rsqrt(jnp.mean(x * x, axis=-1, keepdims=True) + EPS) * g


def _dot(a, b):
    return jnp.dot(a, b, preferred_element_type=F32)


def _sigmoid(x):
    return 0.5 * jnp.tanh(0.5 * x) + 0.5


def _sqrt_nonneg(v):
    return jnp.where(v > 0.0, v * lax.rsqrt(v), 0.0)


def _softplus(x):
    return jnp.maximum(x, 0.0) + jnp.log1p(jnp.exp(-jnp.abs(x)))


class _Packed:
    def __init__(self, named):
        self.offsets = {}
        parts, off = [], 0
        for name, v in named:
            v = v.reshape(-1).astype(F32)
            assert v.shape[0] % LANES == 0
            self.offsets[name] = (off, v.shape[0])
            parts.append(v)
            off += v.shape[0]
        self.array = jnp.concatenate(parts).reshape(1, off)

    def view(self, ref):
        offsets = self.offsets

        def get(name, lo=0, width=None):
            off, n = offsets[name]
            width = n - lo if width is None else width
            return ref[:, off + lo:off + lo + width]
        return get


def _stream_cast(jobs, stage, sem, stage_rows):
    chunks = []
    for src, dst in jobs:
        rows, cols = dst.shape
        rc = min(rows, stage_rows)
        assert rows % rc == 0 and cols <= stage.shape[2]
        for k in range(rows // rc):
            chunks.append((src, dst, k * rc, rc, cols))

    def copy(i):
        src, _, r0, rc, cols = chunks[i]
        slot = i % 2
        return pltpu.make_async_copy(
            src.at[pl.ds(r0, rc), :],
            stage.at[slot, pl.ds(0, rc), pl.ds(0, cols)],
            sem.at[slot])

    copy(0).start()
    for i in range(len(chunks)):
        if i + 1 < len(chunks):
            copy(i + 1).start()
        copy(i).wait()
        _, dst, r0, rc, cols = chunks[i]
        dst[pl.ds(r0, rc), :] = stage[i % 2, pl.ds(0, rc), pl.ds(0, cols)].astype(BF16)


def _mixer_kernel(vec, x_ref, vec_ref, win_hbm, wglu_hbm, waout_hbm, wbout_hbm,
                  wo_hbm, bbd_ref, ccat_ref, wg_ref, o_ref,
                  win, wglu, waout, wbout, wo, stage, sem,
                  xtb, gates, ua, ubuf, xr, xi, s5st, abuf, bxbuf, lrust):
    nb, tq, d = x_ref.shape
    m = nb * tq
    s5w = ua.shape[1]
    nst = xr.shape[1]
    half_w = s5w // 2
    half_st = nst // 2
    lw = abuf.shape[1]
    hdr = (CONV_WIDTH - 1) * nb
    v = vec.view(vec_ref)

    @pl.when(pl.program_id(0) == 0)
    def _():
        _stream_cast([(win_hbm.at[0], win), (wglu_hbm.at[0], wglu),
                      (waout_hbm.at[0], waout), (wbout_hbm.at[0], wbout),
                      (wo_hbm.at[0], wo)], stage, sem, MIXER_STAGE_ROWS)
        s5st[...] = jnp.zeros_like(s5st)
        lrust[...] = jnp.zeros_like(lrust)
        ubuf[pl.ds(0, hdr), :] = jnp.zeros((hdr, lw), F32)

    xtb[...] = jnp.swapaxes(x_ref[...], 0, 1).reshape(m, d)

    sp = LRU_C * _softplus(-v("lru_lambda"))
    h = _rms(xtb[...], v("g_mix")).astype(BF16)
    g0 = s5w + lw
    n_blk = lw // MXU_DIM
    gate_w = 2 * d // n_blk

    def in_proj(lo, width):
        return _dot(h, win[:, lo:lo + width]) + v("b_in", lo, width)

    u = in_proj(0, s5w)
    ua[...] = u
    ub16 = u.astype(BF16)
    for j in range(n_blk):
        cs = slice(j * MXU_DIM, (j + 1) * MXU_DIM)
        ub = in_proj(s5w + j * MXU_DIM, MXU_DIM)
        ubuf[pl.ds(hdr, m), cs] = ub
        xc = (v("conv_b", j * MXU_DIM, MXU_DIM)
              + v("conv_w", (CONV_WIDTH - 1) * lw + j * MXU_DIM, MXU_DIM) * ub)
        for k in range(CONV_WIDTH - 1):
            xc = xc + (v("conv_w", k * lw + j * MXU_DIM, MXU_DIM)
                       * ubuf[pl.ds(k * nb, m), cs])
        if j < 2:
            pr = _dot(ub16[:, j * half_w:(j + 1) * half_w], bbd_ref[j])
            xr[:, j * half_st:(j + 1) * half_st] = pr[:, :half_st]
            xi[:, j * half_st:(j + 1) * half_st] = pr[:, half_st:]
        g = _dot(xc.astype(BF16), wg_ref[j])
        r = _sigmoid(g[:, :MXU_DIM] + v("b_r", j * MXU_DIM, MXU_DIM))
        ig = _sigmoid(g[:, MXU_DIM:] + v("b_i", j * MXU_DIM, MXU_DIM))
        a = jnp.exp(-(r * sp[:, cs]))
        abuf[:, cs] = a
        bxbuf[:, cs] = _sqrt_nonneg(1.0 - a * a) * ig * xc
        gates[:, j * gate_w:(j + 1) * gate_w] = _sigmoid(
            in_proj(g0 + j * gate_w, gate_w)).astype(BF16)
    ubuf[pl.ds(0, hdr), :] = ubuf[pl.ds(m, hdr), :]

    n_pass = nst // SCAN_LANES
    lru_lanes = lw // n_pass
    for c in range(n_pass):
        ls = slice(c * SCAN_LANES, (c + 1) * SCAN_LANES)
        ll = slice(c * lru_lanes, (c + 1) * lru_lanes)
        ar = jnp.broadcast_to(v("a_re", c * SCAN_LANES, SCAN_LANES), (nb, SCAN_LANES))
        ai = jnp.broadcast_to(v("a_im", c * SCAN_LANES, SCAN_LANES), (nb, SCAN_LANES))

        def scan_step(t, carry, ls=ls, ll=ll, ar=ar, ai=ai):
            sr, si, hl = carry
            rows = pl.ds(pl.multiple_of(t * nb, nb), nb)
            nsr = ar * sr - ai * si + xr[rows, ls]
            nsi = ar * si + ai * sr + xi[rows, ls]
            xr[rows, ls] = nsr
            xi[rows, ls] = nsi
            hn = abuf[rows, ll] * hl + bxbuf[rows, ll]
            bxbuf[rows, ll] = hn
            return nsr, nsi, hn

        sr, si, hl = lax.fori_loop(
            0, tq, scan_step, (s5st[0, :, ls], s5st[1, :, ls], lrust[:, ll]),
            unroll=SCAN_UNROLL)
        s5st[0, :, ls] = sr
        s5st[1, :, ls] = si
        lrust[:, ll] = hl

    ys = []
    for hh in range(2):
        ls = slice(hh * half_st, (hh + 1) * half_st)
        ys.append(_dot(xr[:, ls].astype(BF16), ccat_ref[hh, 0])
                  + _dot(xi[:, ls].astype(BF16), ccat_ref[hh, 1]))
    y = jnp.concatenate(ys, axis=1) + v("s5_d") * ua[...]
    z = jax.nn.gelu(y)
    y_a = z * _sigmoid(_dot(z.astype(BF16), wglu[...]) + v("b_glu"))
    merged = gates[:, 0:d] * _dot(y_a.astype(BF16), waout[...])
    merged = merged + gates[:, d:2 * d] * _dot(bxbuf[...].astype(BF16), wbout[...])
    x1 = xtb[...] + _dot(merged.astype(BF16), wo[...])

    o_ref[...] = jnp.swapaxes(x1.reshape(tq, nb, d), 0, 1)


def _const_spec(shape):
    nd = len(shape)
    return pl.BlockSpec(shape, lambda i, _nd=nd: (0,) * _nd,
                        pipeline_mode=pl.Buffered(1))


def _mixer_call(x, vec, dense, bbd, ccat, wg):
    nb, seq, d = x.shape
    m = nb * TQ
    w_in, w_glu, w_a_out, w_b_out, w_o = dense
    s5w = w_glu.shape[1]
    nst = ccat.shape[2] * 2
    lw = w_b_out.shape[1]
    stage_cols = max(w.shape[2] for w in dense)
    in_specs = [pl.BlockSpec((nb, TQ, d), lambda i: (0, i, 0)),
                _const_spec(vec.array.shape)]
    in_specs += [pl.BlockSpec(memory_space=pl.ANY) for _ in dense]
    in_specs += [_const_spec(c.shape) for c in (bbd, ccat, wg)]
    scratch = [pltpu.VMEM(w.shape[1:], BF16) for w in dense]
    scratch += [
        pltpu.VMEM((2, MIXER_STAGE_ROWS, stage_cols), F32),
        pltpu.SemaphoreType.DMA((2,)),
        pltpu.VMEM((m, d), F32),
        pltpu.VMEM((m, 2 * d), BF16),
        pltpu.VMEM((m, s5w), F32),
        pltpu.VMEM((m + (CONV_WIDTH - 1) * nb, lw), F32),
        pltpu.VMEM((m, nst), F32),
        pltpu.VMEM((m, nst), F32),
        pltpu.VMEM((2, nb, nst), F32),
        pltpu.VMEM((m, lw), F32),
        pltpu.VMEM((m, lw), F32),
        pltpu.VMEM((nb, lw), F32),
    ]
    return pl.pallas_call(
        functools.partial(_mixer_kernel, vec),
        out_shape=jax.ShapeDtypeStruct(x.shape, F32),
        grid=(seq // TQ,),
        in_specs=in_specs,
        out_specs=pl.BlockSpec((nb, TQ, d), lambda i: (0, i, 0)),
        scratch_shapes=scratch,
        compiler_params=pltpu.CompilerParams(
            dimension_semantics=("arbitrary",), vmem_limit_bytes=VMEM_LIMIT),
        name="mixer",
    )(x, vec.array, *dense, bbd, ccat, wg)


def _ffn_kernel(vec, x_ref, p_ref, vec_ref, wgate_hbm, wup_hbm, wdown_hbm,
                wpg_hbm, wple_hbm, o_ref,
                wgate, wup, wdown, wpg, wple, stage, sem):
    v = vec.view(vec_ref)

    @pl.when(pl.program_id(0) == 0)
    def _():
        _stream_cast([(wgate_hbm.at[0], wgate), (wup_hbm.at[0], wup),
                      (wdown_hbm.at[0], wdown), (wpg_hbm.at[0], wpg),
                      (wple_hbm.at[0], wple)], stage, sem, FFN_STAGE_ROWS)

    hid = wgate.shape[1]
    for s in range(FFN_ROWS // FFN_SUB_ROWS):
        rows = pl.ds(s * FFN_SUB_ROWS, FFN_SUB_ROWS)
        x = x_ref[rows, :]
        h2 = _rms(x, v("g_ffn")).astype(BF16)
        acc = x
        for j in range(hid // FFN_HID_BLK):
            cs = slice(j * FFN_HID_BLK, (j + 1) * FFN_HID_BLK)
            gt = _dot(h2, wgate[:, cs])
            a = gt * _sigmoid(gt) * _dot(h2, wup[:, cs])
            acc = acc + _dot(a.astype(BF16), wdown[cs, :])
        x2 = acc
        gate_p = _sigmoid(
            _dot(_rms(x2, v("g_ple_gate")).astype(BF16), wpg[...]) + v("b_ple_gate"))
        e = _rms(_dot(p_ref[rows, :].astype(BF16), wple[...]), v("g_ple"))
        x3 = x2 + gate_p * e
        o_ref[rows, :] = _rms(x3, v("g_final"))


def _ffn_call(x1, p, vec, dense):
    bsz, seq, d = x1.shape
    pd = p.shape[-1]
    per_seq = seq // FFN_ROWS
    stage_cols = max(w.shape[2] for w in dense)
    in_specs = [
        pl.BlockSpec((None, FFN_ROWS, d), lambda i: (i // per_seq, i % per_seq, 0)),
        pl.BlockSpec((None, None, FFN_ROWS, pd),
                     lambda i: (0, i // per_seq, i % per_seq, 0)),
        _const_spec(vec.array.shape)]
    in_specs += [pl.BlockSpec(memory_space=pl.ANY) for _ in dense]
    scratch = [pltpu.VMEM(w.shape[1:], BF16) for w in dense]
    scratch += [pltpu.VMEM((2, FFN_STAGE_ROWS, stage_cols), F32),
                pltpu.SemaphoreType.DMA((2,))]
    return pl.pallas_call(
        functools.partial(_ffn_kernel, vec),
        out_shape=jax.ShapeDtypeStruct((bsz, seq, d), F32),
        grid=(bsz * per_seq,),
        in_specs=in_specs,
        out_specs=pl.BlockSpec((None, FFN_ROWS, d),
                               lambda i: (i // per_seq, i % per_seq, 0)),
        scratch_shapes=scratch,
        compiler_params=pltpu.CompilerParams(
            dimension_semantics=("arbitrary",), vmem_limit_bytes=VMEM_LIMIT),
        name="ffn",
    )(x1, p, vec.array, *dense)


def _s5_params(lam_re, lam_im, log_dt, b_re, b_im, c_re, c_im):
    g, n = lam_re.shape
    p = b_re.shape[2]
    hg = g // 2
    dt = jnp.exp(log_dt)[:, None]
    mag = jnp.exp(lam_re * dt)
    ar = mag * jnp.cos(lam_im * dt)
    ai = mag * jnp.sin(lam_im * dt)
    den = lam_re * lam_re + lam_im * lam_im
    nr = ar - 1.0
    fr = (nr * lam_re + ai * lam_im) / den
    fi = (ai * lam_re - nr * lam_im) / den
    bbr = fr[..., None] * b_re - fi[..., None] * b_im
    bbi = fr[..., None] * b_im + fi[..., None] * b_re
    eye = jnp.eye(hg, dtype=F32)
    bb = jnp.stack([bbr, bbi]).reshape(2, 2, hg, n, p).transpose(1, 2, 4, 0, 3)
    bbd = (bb[:, :, :, :, None, :] * eye[None, :, None, None, :, None])
    bbd = bbd.reshape(2, hg * p, 2 * hg * n).astype(BF16)
    cc = jnp.stack([c_re, -c_im]).reshape(2, 2, hg, p, n).transpose(1, 0, 2, 4, 3)
    ccat = (cc[:, :, :, :, None, :] * eye[None, None, :, None, :, None])
    ccat = ccat.reshape(2, 2, hg * n, hg * p).astype(BF16)
    return ar.reshape(-1), ai.reshape(-1), bbd, ccat


def _lru_gate_weights(w_r, w_i):
    heads, hd, _ = w_r.shape
    per = MXU_DIM // hd
    tiles = heads // per
    eye = jnp.eye(per, dtype=F32)
    w = jnp.stack([w_r, w_i]).reshape(2, tiles, per, hd, hd).transpose(1, 2, 3, 0, 4)
    wbd = w[:, :, :, :, None, :] * eye[None, :, None, None, :, None]
    return wbd.reshape(tiles, per * hd, 2 * per * hd).astype(BF16)


def kernel(x, p, g_mix, w_in, b_in, lam_re, lam_im, log_dt, s5_b_re, s5_b_im, s5_c_re, s5_c_im, s5_d, w_glu, b_glu, conv_w, conv_b, w_r, b_r, w_i, b_i, lru_lambda, w_a_out, w_b_out, w_o, g_ffn, w_ffn_gate, w_ffn_up, w_ffn_down, g_ple_gate, w_ple_gate, b_ple_gate, w_ple, g_ple, g_final):
    bsz, seq, d = x.shape
    assert w_in.shape[0] == 1
    assert bsz == SUBLANES and seq % TQ == 0 and seq % FFN_ROWS == 0

    a_re, a_im, bbd, ccat = _s5_params(
        lam_re[0], lam_im[0], log_dt[0], s5_b_re[0], s5_b_im[0],
        s5_c_re[0], s5_c_im[0])
    mixer_vec = _Packed([
        ("g_mix", g_mix), ("b_in", b_in), ("a_re", a_re), ("a_im", a_im),
        ("s5_d", s5_d), ("b_glu", b_glu), ("conv_w", conv_w), ("conv_b", conv_b),
        ("b_r", b_r), ("b_i", b_i), ("lru_lambda", lru_lambda)])
    x1 = _mixer_call(x, mixer_vec, (w_in, w_glu, w_a_out, w_b_out, w_o),
                     bbd, ccat, _lru_gate_weights(w_r[0], w_i[0]))
    ffn_vec = _Packed([
        ("g_ffn", g_ffn), ("g_ple_gate", g_ple_gate), ("b_ple_gate", b_ple_gate),
        ("g_ple", g_ple), ("g_final", g_final)])
    return _ffn_call(x1, p, ffn_vec,
                     (w_ffn_gate, w_ffn_up, w_ffn_down, w_ple_gate, w_ple))
```

```python
import functools

import jax
import jax.numpy as jnp
from jax import lax
from jax.experimental import pallas as pl
from jax.experimental.pallas import tpu as pltpu

F32 = jnp.float32
BF16 = jnp.bfloat16

EPS = 1e-6
LRU_C = 8.0
CONV_WIDTH = 4
LRU_HEAD_DIM = 64
S5_GROUP_CH = 16

SUBLANES = 8
LANES = 128
MXU_DIM = 256
TQ = 64
SCAN_LANES = 512
SCAN_UNROLL = 8
STAGE_SLOTS = 4
MIXER_STAGE_ROWS = 64
FFN_ROWS = 1024
FFN_SUB_ROWS = 512
FFN_HID_BLK = 256
FFN_STAGE_ROWS = 128
VMEM_LIMIT = 56 * 1024 * 1024


def _rms(x, g):
    return x * lax.rsqrt(jnp.mean(x * x, axis=-1, keepdims=True) + EPS) * g


def _dot(a, b):
    return jnp.dot(a, b, preferred_element_type=F32)


def _sigmoid(x):
    return 0.5 * jnp.tanh(0.5 * x) + 0.5


def _sqrt_nonneg(v):
    return jnp.where(v > 0.0, v * lax.rsqrt(v), 0.0)


def _softplus(x):
    return jnp.maximum(x, 0.0) + jnp.log1p(jnp.exp(-jnp.abs(x)))


class _Packed:
    def __init__(self, named):
        self.offsets = {}
        parts, off = [], 0
        for name, v in named:
            v = v.reshape(-1).astype(F32)
            assert v.shape[0] % LANES == 0
            self.offsets[name] = (off, v.shape[0])
            parts.append(v)
            off += v.shape[0]
        self.array = jnp.concatenate(parts).reshape(1, off)

    def view(self, ref):
        offsets = self.offsets

        def get(name, lo=0, width=None):
            off, n = offsets[name]
            width = n - lo if width is None else width
            return ref[:, off + lo:off + lo + width]
        return get


def _stream_cast(jobs, stage, sem):
    n_slots, stage_rows, stage_cols = stage.shape
    chunks = []
    for src, dst in jobs:
        rows, cols = dst.shape
        rc = min(rows, stage_rows)
        assert rows % rc == 0 and cols <= stage_cols
        for k in range(rows // rc):
            chunks.append((src, dst, k * rc, rc, cols))

    def copy(i):
        src, _, r0, rc, cols = chunks[i]
        slot = i % n_slots
        return pltpu.make_async_copy(
            src.at[pl.ds(r0, rc), :],
            stage.at[slot, pl.ds(0, rc), pl.ds(0, cols)],
            sem.at[slot])

    ahead = n_slots - 1
    for i in range(min(ahead, len(chunks))):
        copy(i).start()
    for i in range(len(chunks)):
        if i + ahead < len(chunks):
            copy(i + ahead).start()
        copy(i).wait()
        _, dst, r0, rc, cols = chunks[i]
        dst[pl.ds(r0, rc), :] = (
            stage[i % n_slots, pl.ds(0, rc), pl.ds(0, cols)].astype(BF16))


def _pow2_div(x, n):
    assert n & (n - 1) == 0
    return x >> (n.bit_length() - 1)


def _pow2_mod(x, n):
    assert n & (n - 1) == 0
    return x & (n - 1)


def _block_diag_cols(compact, row_blk, col_blk, n_blocks):
    rows, ccols = compact.shape
    n_parts = ccols // col_blk
    part_w = n_blocks * col_blk
    ocols = n_parts * part_w
    k_i = lax.broadcasted_iota(jnp.int32, (ccols, ocols), 0)
    c_i = lax.broadcasted_iota(jnp.int32, (ccols, ocols), 1)
    onehot = ((_pow2_div(k_i, col_blk) == _pow2_div(c_i, part_w))
              & (_pow2_mod(k_i, col_blk) == _pow2_mod(c_i, col_blk)))
    tiled = _dot(compact.astype(BF16), onehot.astype(BF16))
    r_o = lax.broadcasted_iota(jnp.int32, (rows, ocols), 0)
    c_o = lax.broadcasted_iota(jnp.int32, (rows, ocols), 1)
    keep = _pow2_div(r_o, row_blk) == _pow2_div(_pow2_mod(c_o, part_w), col_blk)
    return jnp.where(keep, tiled, 0.0).astype(BF16)


def _block_diag_rows(compact, row_blk, col_blk, n_blocks):
    tiled = jnp.concatenate([compact] * n_blocks, axis=0)
    r_o = lax.broadcasted_iota(jnp.int32, tiled.shape, 0)
    c_o = lax.broadcasted_iota(jnp.int32, tiled.shape, 1)
    keep = _pow2_div(r_o, row_blk) == _pow2_div(c_o, col_blk)
    return jnp.where(keep, tiled, 0.0).astype(BF16)


def _mixer_kernel(vec, x_ref, vec_ref, win_hbm, wglu_hbm, waout_hbm, wbout_hbm,
                  wo_hbm, bc_ref, cc_ref, wc_ref, o_ref,
                  win, wglu, waout, wbout, wo, stage, sem, bbd_ref, ccat_ref, wg_ref,
                  xtb, gates, ua, ubuf, xr, xi, s5st, abuf, bxbuf, lrust):
    nb, tq, d = x_ref.shape
    m = nb * tq
    s5w = ua.shape[1]
    nst = xr.shape[1]
    half_w = s5w // 2
    half_st = nst // 2
    lw = abuf.shape[1]
    hdr = (CONV_WIDTH - 1) * nb
    v = vec.view(vec_ref)

    @pl.when(pl.program_id(0) == 0)
    def _():
        _stream_cast([(win_hbm.at[0], win), (wglu_hbm.at[0], wglu),
                      (waout_hbm.at[0], waout), (wbout_hbm.at[0], wbout),
                      (wo_hbm.at[0], wo)], stage, sem)
        s5_p = S5_GROUP_CH
        s5_n = cc_ref.shape[2]
        n_grp = bc_ref.shape[1] // s5_p
        for hh in range(bbd_ref.shape[0]):
            bbd_ref[hh] = _block_diag_cols(bc_ref[hh], s5_p, s5_n, n_grp)
            for ri in range(2):
                ccat_ref[hh, ri] = _block_diag_rows(cc_ref[hh, ri], s5_n, s5_p, n_grp)
        for j in range(wg_ref.shape[0]):
            wg_ref[j] = _block_diag_cols(wc_ref[j], LRU_HEAD_DIM, LRU_HEAD_DIM,
                                         MXU_DIM // LRU_HEAD_DIM)
        s5st[...] = jnp.zeros_like(s5st)
        lrust[...] = jnp.zeros_like(lrust)
        ubuf[pl.ds(0, hdr), :] = jnp.zeros((hdr, lw), F32)

    xtb[...] = jnp.swapaxes(x_ref[...], 0, 1).reshape(m, d)

    sp = LRU_C * _softplus(-v("lru_lambda"))
    h = _rms(xtb[...], v("g_mix")).astype(BF16)
    g0 = s5w + lw
    n_blk = lw // MXU_DIM
    gate_w = 2 * d // n_blk

    def in_proj(lo, width):
        return _dot(h, win[:, lo:lo + width]) + v("b_in", lo, width)

    u = in_proj(0, s5w)
    ua[...] = u
    ub16 = u.astype(BF16)
    for j in range(n_blk):
        cs = slice(j * MXU_DIM, (j + 1) * MXU_DIM)
        ub = in_proj(s5w + j * MXU_DIM, MXU_DIM)
        ubuf[pl.ds(hdr, m), cs] = ub
        xc = (v("conv_b", j * MXU_DIM, MXU_DIM)
              + v("conv_w", (CONV_WIDTH - 1) * lw + j * MXU_DIM, MXU_DIM) * ub)
        for k in range(CONV_WIDTH - 1):
            xc = xc + (v("conv_w", k * lw + j * MXU_DIM, MXU_DIM)
                       * ubuf[pl.ds(k * nb, m), cs])
        if j < 2:
            pr = _dot(ub16[:, j * half_w:(j + 1) * half_w], bbd_ref[j])
            xr[:, j * half_st:(j + 1) * half_st] = pr[:, :half_st]
            xi[:, j * half_st:(j + 1) * half_st] = pr[:, half_st:]
        g = _dot(xc.astype(BF16), wg_ref[j])
        r = _sigmoid(g[:, :MXU_DIM] + v("b_r", j * MXU_DIM, MXU_DIM))
        ig = _sigmoid(g[:, MXU_DIM:] + v("b_i", j * MXU_DIM, MXU_DIM))
        a = jnp.exp(-(r * sp[:, cs]))
        abuf[:, cs] = a
        bxbuf[:, cs] = _sqrt_nonneg(1.0 - a * a) * ig * xc
        gates[:, j * gate_w:(j + 1) * gate_w] = _sigmoid(
            in_proj(g0 + j * gate_w, gate_w)).astype(BF16)
    ubuf[pl.ds(0, hdr), :] = ubuf[pl.ds(m, hdr), :]

    n_pass = nst // SCAN_LANES
    lru_lanes = lw // n_pass
    for c in range(n_pass):
        ls = slice(c * SCAN_LANES, (c + 1) * SCAN_LANES)
        ll = slice(c * lru_lanes, (c + 1) * lru_lanes)
        ar = jnp.broadcast_to(v("a_re", c * SCAN_LANES, SCAN_LANES), (nb, SCAN_LANES))
        ai = jnp.broadcast_to(v("a_im", c * SCAN_LANES, SCAN_LANES), (nb, SCAN_LANES))

        def scan_step(t, carry, ls=ls, ll=ll, ar=ar, ai=ai):
            sr, si, hl = carry
            rows = pl.ds(pl.multiple_of(t * nb, nb), nb)
            nsr = ar * sr - ai * si + xr[rows, ls]
            nsi = ar * si + ai * sr + xi[rows, ls]
            xr[rows, ls] = nsr
            xi[rows, ls] = nsi
            hn = abuf[rows, ll] * hl + bxbuf[rows, ll]
            bxbuf[rows, ll] = hn
            return nsr, nsi, hn

        sr, si, hl = lax.fori_loop(
            0, tq, scan_step, (s5st[0, :, ls], s5st[1, :, ls], lrust[:, ll]),
            unroll=SCAN_UNROLL)
        s5st[0, :, ls] = sr
        s5st[1, :, ls] = si
        lrust[:, ll] = hl

    ys = []
    for hh in range(2):
        ls = slice(hh * half_st, (hh + 1) * half_st)
        ys.append(_dot(xr[:, ls].astype(BF16), ccat_ref[hh, 0])
                  + _dot(xi[:, ls].astype(BF16), ccat_ref[hh, 1]))
    y = jnp.concatenate(ys, axis=1) + v("s5_d") * ua[...]
    z = jax.nn.gelu(y)
    y_a = z * _sigmoid(_dot(z.astype(BF16), wglu[...]) + v("b_glu"))
    merged = gates[:, 0:d] * _dot(y_a.astype(BF16), waout[...])
    merged = merged + gates[:, d:2 * d] * _dot(bxbuf[...].astype(BF16), wbout[...])
    x1 = xtb[...] + _dot(merged.astype(BF16), wo[...])

    o_ref[...] = jnp.swapaxes(x1.reshape(tq, nb, d), 0, 1)


def _const_spec(shape):
    nd = len(shape)
    return pl.BlockSpec(shape, lambda i, _nd=nd: (0,) * _nd,
                        pipeline_mode=pl.Buffered(1))


def _mixer_call(x, vec, dense, bc, cc, wc):
    nb, seq, d = x.shape
    m = nb * TQ
    w_in, w_glu, w_a_out, w_b_out, w_o = dense
    s5w = w_glu.shape[1]
    n_half, half_w, _ = bc.shape
    half_st = half_w // S5_GROUP_CH * cc.shape[2]
    nst = n_half * half_st
    lw = w_b_out.shape[1]
    stage_cols = max(w.shape[2] for w in dense)
    in_specs = [pl.BlockSpec((nb, TQ, d), lambda i: (0, i, 0)),
                _const_spec(vec.array.shape)]
    in_specs += [pl.BlockSpec(memory_space=pl.ANY) for _ in dense]
    in_specs += [_const_spec(c.shape) for c in (bc, cc, wc)]
    scratch = [pltpu.VMEM(w.shape[1:], BF16) for w in dense]
    scratch += [
        pltpu.VMEM((STAGE_SLOTS, MIXER_STAGE_ROWS, stage_cols), F32),
        pltpu.SemaphoreType.DMA((STAGE_SLOTS,)),
        pltpu.VMEM((n_half, half_w, 2 * half_st), BF16),
        pltpu.VMEM((n_half, 2, half_st, half_w), BF16),
        pltpu.VMEM((wc.shape[0], MXU_DIM, 2 * MXU_DIM), BF16),
        pltpu.VMEM((m, d), F32),
        pltpu.VMEM((m, 2 * d), BF16),
        pltpu.VMEM((m, s5w), F32),
        pltpu.VMEM((m + (CONV_WIDTH - 1) * nb, lw), F32),
        pltpu.VMEM((m, nst), F32),
        pltpu.VMEM((m, nst), F32),
        pltpu.VMEM((2, nb, nst), F32),
        pltpu.VMEM((m, lw), F32),
        pltpu.VMEM((m, lw), F32),
        pltpu.VMEM((nb, lw), F32),
    ]
    return pl.pallas_call(
        functools.partial(_mixer_kernel, vec),
        out_shape=jax.ShapeDtypeStruct(x.shape, F32),
        grid=(seq // TQ,),
        in_specs=in_specs,
        out_specs=pl.BlockSpec((nb, TQ, d), lambda i: (0, i, 0)),
        scratch_shapes=scratch,
        compiler_params=pltpu.CompilerParams(
            dimension_semantics=("arbitrary",), vmem_limit_bytes=VMEM_LIMIT),
        name="mixer",
    )(x, vec.array, *dense, bc, cc, wc)


def _ffn_kernel(vec, x_ref, p_ref, vec_ref, wgate_hbm, wup_hbm, wdown_hbm,
                wpg_hbm, wple_hbm, o_ref,
                wgate, wup, wdown, wpg, wple, stage, sem):
    v = vec.view(vec_ref)

    @pl.when(pl.program_id(0) == 0)
    def _():
        _stream_cast([(wgate_hbm.at[0], wgate), (wup_hbm.at[0], wup),
                      (wdown_hbm.at[0], wdown), (wpg_hbm.at[0], wpg),
                      (wple_hbm.at[0], wple)], stage, sem)

    hid = wgate.shape[1]
    for s in range(FFN_ROWS // FFN_SUB_ROWS):
        rows = pl.ds(s * FFN_SUB_ROWS, FFN_SUB_ROWS)
        x = x_ref[rows, :]
        h2 = _rms(x, v("g_ffn")).astype(BF16)
        acc = x
        for j in range(hid // FFN_HID_BLK):
            cs = slice(j * FFN_HID_BLK, (j + 1) * FFN_HID_BLK)
            gt = _dot(h2, wgate[:, cs])
            a = gt * _sigmoid(gt) * _dot(h2, wup[:, cs])
            acc = acc + _dot(a.astype(BF16), wdown[cs, :])
        x2 = acc
        gate_p = _sigmoid(
            _dot(_rms(x2, v("g_ple_gate")).astype(BF16), wpg[...]) + v("b_ple_gate"))
        e = _rms(_dot(p_ref[rows, :].astype(BF16), wple[...]), v("g_ple"))
        x3 = x2 + gate_p * e
        o_ref[rows, :] = _rms(x3, v("g_final"))


def _ffn_call(x1, p, vec, dense):
    bsz, seq, d = x1.shape
    pd = p.shape[-1]
    per_seq = seq // FFN_ROWS
    stage_cols = max(w.shape[2] for w in dense)
    in_specs = [
        pl.BlockSpec((None, FFN_ROWS, d), lambda i: (i // per_seq, i % per_seq, 0)),
        pl.BlockSpec((None, None, FFN_ROWS, pd),
                     lambda i: (0, i // per_seq, i % per_seq, 0)),
        _const_spec(vec.array.shape)]
    in_specs += [pl.BlockSpec(memory_space=pl.ANY) for _ in dense]
    scratch = [pltpu.VMEM(w.shape[1:], BF16) for w in dense]
    scratch += [pltpu.VMEM((STAGE_SLOTS, FFN_STAGE_ROWS, stage_cols), F32),
                pltpu.SemaphoreType.DMA((STAGE_SLOTS,))]
    return pl.pallas_call(
        functools.partial(_ffn_kernel, vec),
        out_shape=jax.ShapeDtypeStruct((bsz, seq, d), F32),
        grid=(bsz * per_seq,),
        in_specs=in_specs,
        out_specs=pl.BlockSpec((None, FFN_ROWS, d),
                               lambda i: (i // per_seq, i % per_seq, 0)),
        scratch_shapes=scratch,
        compiler_params=pltpu.CompilerParams(
            dimension_semantics=("arbitrary",), vmem_limit_bytes=VMEM_LIMIT),
        name="ffn",
    )(x1, p, vec.array, *dense)


def _s5_params(lam_re, lam_im, log_dt, b_re, b_im, c_re, c_im):
    g, n = lam_re.shape
    p = b_re.shape[2]
    hg = g // 2
    dt = jnp.exp(log_dt)[:, None]
    mag = jnp.exp(lam_re * dt)
    ar = mag * jnp.cos(lam_im * dt)
    ai = mag * jnp.sin(lam_im * dt)
    den = lam_re * lam_re + lam_im * lam_im
    nr = ar - 1.0
    fr = (nr * lam_re + ai * lam_im) / den
    fi = (ai * lam_re - nr * lam_im) / den
    bbr = fr[..., None] * b_re - fi[..., None] * b_im
    bbi = fr[..., None] * b_im + fi[..., None] * b_re
    bc = jnp.stack([bbr, bbi]).reshape(2, 2, hg, n, p).transpose(1, 2, 4, 0, 3)
    bc = bc.reshape(2, hg * p, 2 * n)
    cc = jnp.stack([c_re, -c_im]).reshape(2, 2, hg, p, n).transpose(1, 0, 4, 2, 3)
    cc = cc.reshape(2, 2, n, hg * p)
    return ar.reshape(-1), ai.reshape(-1), bc, cc


def _lru_gate_weights(w_r, w_i):
    heads, hd, _ = w_r.shape
    per = MXU_DIM // hd
    tiles = heads // per
    w = jnp.stack([w_r, w_i]).reshape(2, tiles, per, hd, hd).transpose(1, 2, 3, 0, 4)
    return w.reshape(tiles, per * hd, 2 * hd)


def kernel(x, p, g_mix, w_in, b_in, lam_re, lam_im, log_dt, s5_b_re, s5_b_im, s5_c_re, s5_c_im, s5_d, w_glu, b_glu, conv_w, conv_b, w_r, b_r, w_i, b_i, lru_lambda, w_a_out, w_b_out, w_o, g_ffn, w_ffn_gate, w_ffn_up, w_ffn_down, g_ple_gate, w_ple_gate, b_ple_gate, w_ple, g_ple, g_final):
    bsz, seq, d = x.shape
    assert w_in.shape[0] == 1
    assert bsz == SUBLANES and seq % TQ == 0 and seq % FFN_ROWS == 0

    a_re, a_im, bc, cc = _s5_params(
        lam_re[0], lam_im[0], log_dt[0], s5_b_re[0], s5_b_im[0],
        s5_c_re[0], s5_c_im[0])
    mixer_vec = _Packed([
        ("g_mix", g_mix), ("b_in", b_in), ("a_re", a_re), ("a_im", a_im),
        ("s5_d", s5_d), ("b_glu", b_glu), ("conv_w", conv_w), ("conv_b", conv_b),
        ("b_r", b_r), ("b_i", b_i), ("lru_lambda", lru_lambda)])
    x1 = _mixer_call(x, mixer_vec, (w_in, w_glu, w_a_out, w_b_out, w_o),
                     bc, cc, _lru_gate_weights(w_r[0], w_i[0]))
    ffn_vec = _Packed([
        ("g_ffn", g_ffn), ("g_ple_gate", g_ple_gate), ("b_ple_gate", b_ple_gate),
        ("g_ple", g_ple), ("g_final", g_final)])
    return _ffn_call(x1, p, ffn_vec,
                     (w_ffn_gate, w_ffn_up, w_ffn_down, w_ple_gate, w_ple))
```

```python
import functools

import jax
import jax.numpy as jnp
from jax import lax
from jax.experimental import pallas as pl
from jax.experimental.pallas import tpu as pltpu

F32 = jnp.float32
BF16 = jnp.bfloat16

EPS = 1e-6
LRU_C = 8.0
CONV_WIDTH = 4
LRU_HEAD_DIM = 64
S5_GROUP_CH = 16

SUBLANES = 8
LANES = 128
MXU_DIM = 256
TQ = 64
SCAN_LANES = 512
SCAN_UNROLL = 64
STAGE_SLOTS = 4
MIXER_STAGE_ROWS = 64
FFN_ROWS = 1024
FFN_SUB_ROWS = 512
FFN_HID_BLK = 256
FFN_STAGE_ROWS = 128
VMEM_LIMIT = 56 * 1024 * 1024


def _rms(x, g):
    return x * lax.rsqrt(jnp.mean(x * x, axis=-1, keepdims=True) + EPS) * g


def _dot(a, b):
    return jnp.dot(a, b, preferred_element_type=F32)


def _sigmoid(x):
    return 0.5 * jnp.tanh(0.5 * x) + 0.5


def _sqrt_nonneg(v):
    return jnp.where(v > 0.0, v * lax.rsqrt(v), 0.0)


def _softplus(x):
    return jnp.maximum(x, 0.0) + jnp.log1p(jnp.exp(-jnp.abs(x)))


class _Packed:
    def __init__(self, named):
        self.offsets = {}
        parts, off = [], 0
        for name, v in named:
            v = v.reshape(-1).astype(F32)
            assert v.shape[0] % LANES == 0
            self.offsets[name] = (off, v.shape[0])
            parts.append(v)
            off += v.shape[0]
        self.array = jnp.concatenate(parts).reshape(1, off)

    def view(self, ref):
        offsets = self.offsets

        def get(name, lo=0, width=None):
            off, n = offsets[name]
            width = n - lo if width is None else width
            return ref[:, off + lo:off + lo + width]
        return get


def _stream_cast(jobs, stage, sem):
    n_slots, stage_rows, stage_cols = stage.shape
    chunks = []
    for src, dst in jobs:
        rows, cols = dst.shape
        rc = min(rows, stage_rows)
        assert rows % rc == 0 and cols <= stage_cols
        for k in range(rows // rc):
            chunks.append((src, dst, k * rc, rc, cols))

    def copy(i):
        src, _, r0, rc, cols = chunks[i]
        slot = i % n_slots
        return pltpu.make_async_copy(
            src.at[pl.ds(r0, rc), :],
            stage.at[slot, pl.ds(0, rc), pl.ds(0, cols)],
            sem.at[slot])

    ahead = n_slots - 1
    for i in range(min(ahead, len(chunks))):
        copy(i).start()
    for i in range(len(chunks)):
        if i + ahead < len(chunks):
            copy(i + ahead).start()
        copy(i).wait()
        _, dst, r0, rc, cols = chunks[i]
        dst[pl.ds(r0, rc), :] = (
            stage[i % n_slots, pl.ds(0, rc), pl.ds(0, cols)].astype(BF16))


def _pow2_div(x, n):
    assert n & (n - 1) == 0
    return x >> (n.bit_length() - 1)


def _pow2_mod(x, n):
    assert n & (n - 1) == 0
    return x & (n - 1)


def _block_diag_cols(compact, row_blk, col_blk, n_blocks):
    rows, ccols = compact.shape
    n_parts = ccols // col_blk
    part_w = n_blocks * col_blk
    ocols = n_parts * part_w
    k_i = lax.broadcasted_iota(jnp.int32, (ccols, ocols), 0)
    c_i = lax.broadcasted_iota(jnp.int32, (ccols, ocols), 1)
    onehot = ((_pow2_div(k_i, col_blk) == _pow2_div(c_i, part_w))
              & (_pow2_mod(k_i, col_blk) == _pow2_mod(c_i, col_blk)))
    tiled = _dot(compact.astype(BF16), onehot.astype(BF16))
    r_o = lax.broadcasted_iota(jnp.int32, (rows, ocols), 0)
    c_o = lax.broadcasted_iota(jnp.int32, (rows, ocols), 1)
    keep = _pow2_div(r_o, row_blk) == _pow2_div(_pow2_mod(c_o, part_w), col_blk)
    return jnp.where(keep, tiled, 0.0).astype(BF16)


def _block_diag_rows(compact, row_blk, col_blk, n_blocks):
    tiled = jnp.concatenate([compact] * n_blocks, axis=0)
    r_o = lax.broadcasted_iota(jnp.int32, tiled.shape, 0)
    c_o = lax.broadcasted_iota(jnp.int32, tiled.shape, 1)
    keep = _pow2_div(r_o, row_blk) == _pow2_div(c_o, col_blk)
    return jnp.where(keep, tiled, 0.0).astype(BF16)


def _mixer_kernel(vec, x_ref, vec_ref, win_hbm, wglu_hbm, waout_hbm, wbout_hbm,
                  wo_hbm, bc_ref, cc_ref, wc_ref, o_ref,
                  win, wglu, waout, wbout, wo, stage, sem, bbd_ref, ccat_ref, wg_ref,
                  xtb, gates, ua, ubuf, xr, xi, s5st, abuf, bxbuf, lrust):
    nb, tq, d = x_ref.shape
    m = nb * tq
    s5w = ua.shape[1]
    nst = xr.shape[1]
    half_w = s5w // 2
    half_st = nst // 2
    lw = abuf.shape[1]
    hdr = (CONV_WIDTH - 1) * nb
    v = vec.view(vec_ref)

    @pl.when(pl.program_id(0) == 0)
    def _():
        _stream_cast([(win_hbm.at[0], win), (wglu_hbm.at[0], wglu),
                      (waout_hbm.at[0], waout), (wbout_hbm.at[0], wbout),
                      (wo_hbm.at[0], wo)], stage, sem)
        s5_p = S5_GROUP_CH
        s5_n = cc_ref.shape[2]
        n_grp = bc_ref.shape[1] // s5_p
        for hh in range(bbd_ref.shape[0]):
            bbd_ref[hh] = _block_diag_cols(bc_ref[hh], s5_p, s5_n, n_grp)
            for ri in range(2):
                ccat_ref[hh, ri] = _block_diag_rows(cc_ref[hh, ri], s5_n, s5_p, n_grp)
        for j in range(wg_ref.shape[0]):
            wg_ref[j] = _block_diag_cols(wc_ref[j], LRU_HEAD_DIM, LRU_HEAD_DIM,
                                         MXU_DIM // LRU_HEAD_DIM)
        s5st[...] = jnp.zeros_like(s5st)
        lrust[...] = jnp.zeros_like(lrust)
        ubuf[pl.ds(0, hdr), :] = jnp.zeros((hdr, lw), F32)

    xtb[...] = jnp.swapaxes(x_ref[...], 0, 1).reshape(m, d)

    sp = LRU_C * _softplus(-v("lru_lambda"))
    h = _rms(xtb[...], v("g_mix")).astype(BF16)
    g0 = s5w + lw
    n_blk = lw // MXU_DIM
    gate_w = 2 * d // n_blk

    def in_proj(lo, width):
        return _dot(h, win[:, lo:lo + width]) + v("b_in", lo, width)

    u = in_proj(0, s5w)
    ua[...] = u
    ub16 = u.astype(BF16)
    for j in range(n_blk):
        cs = slice(j * MXU_DIM, (j + 1) * MXU_DIM)
        ub = in_proj(s5w + j * MXU_DIM, MXU_DIM)
        ubuf[pl.ds(hdr, m), cs] = ub
        xc = (v("conv_b", j * MXU_DIM, MXU_DIM)
              + v("conv_w", (CONV_WIDTH - 1) * lw + j * MXU_DIM, MXU_DIM) * ub)
        for k in range(CONV_WIDTH - 1):
            xc = xc + (v("conv_w", k * lw + j * MXU_DIM, MXU_DIM)
                       * ubuf[pl.ds(k * nb, m), cs])
        if j < 2:
            pr = _dot(ub16[:, j * half_w:(j + 1) * half_w], bbd_ref[j])
            xr[:, j * half_st:(j + 1) * half_st] = pr[:, :half_st]
            xi[:, j * half_st:(j + 1) * half_st] = pr[:, half_st:]
        g = _dot(xc.astype(BF16), wg_ref[j])
        r = _sigmoid(g[:, :MXU_DIM] + v("b_r", j * MXU_DIM, MXU_DIM))
        ig = _sigmoid(g[:, MXU_DIM:] + v("b_i", j * MXU_DIM, MXU_DIM))
        a = jnp.exp(-(r * sp[:, cs]))
        abuf[:, cs] = a
        bxbuf[:, cs] = _sqrt_nonneg(1.0 - a * a) * ig * xc
        gates[:, j * gate_w:(j + 1) * gate_w] = _sigmoid(
            in_proj(g0 + j * gate_w, gate_w)).astype(BF16)
    ubuf[pl.ds(0, hdr), :] = ubuf[pl.ds(m, hdr), :]

    n_pass = nst // SCAN_LANES
    lru_lanes = lw // n_pass
    for c in range(n_pass):
        ls = slice(c * SCAN_LANES, (c + 1) * SCAN_LANES)
        ll = slice(c * lru_lanes, (c + 1) * lru_lanes)
        ar = jnp.broadcast_to(v("a_re", c * SCAN_LANES, SCAN_LANES), (nb, SCAN_LANES))
        ai = jnp.broadcast_to(v("a_im", c * SCAN_LANES, SCAN_LANES), (nb, SCAN_LANES))

        def scan_step(t, carry, ls=ls, ll=ll, ar=ar, ai=ai):
            sr, si, hl = carry
            rows = pl.ds(pl.multiple_of(t * nb, nb), nb)
            nsr = ar * sr - ai * si + xr[rows, ls]
            nsi = ar * si + ai * sr + xi[rows, ls]
            xr[rows, ls] = nsr
            xi[rows, ls] = nsi
            hn = abuf[rows, ll] * hl + bxbuf[rows, ll]
            bxbuf[rows, ll] = hn
            return nsr, nsi, hn

        sr, si, hl = lax.fori_loop(
            0, tq, scan_step, (s5st[0, :, ls], s5st[1, :, ls], lrust[:, ll]),
            unroll=SCAN_UNROLL)
        s5st[0, :, ls] = sr
        s5st[1, :, ls] = si
        lrust[:, ll] = hl

    ys = []
    for hh in range(2):
        ls = slice(hh * half_st, (hh + 1) * half_st)
        ys.append(_dot(xr[:, ls].astype(BF16), ccat_ref[hh, 0])
                  + _dot(xi[:, ls].astype(BF16), ccat_ref[hh, 1]))
    y = jnp.concatenate(ys, axis=1) + v("s5_d") * ua[...]
    z = jax.nn.gelu(y)
    y_a = z * _sigmoid(_dot(z.astype(BF16), wglu[...]) + v("b_glu"))
    merged = gates[:, 0:d] * _dot(y_a.astype(BF16), waout[...])
    merged = merged + gates[:, d:2 * d] * _dot(bxbuf[...].astype(BF16), wbout[...])
    x1 = xtb[...] + _dot(merged.astype(BF16), wo[...])

    o_ref[...] = jnp.swapaxes(x1.reshape(tq, nb, d), 0, 1)


def _const_spec(shape):
    nd = len(shape)
    return pl.BlockSpec(shape, lambda i, _nd=nd: (0,) * _nd,
                        pipeline_mode=pl.Buffered(1))


def _mixer_call(x, vec, dense, bc, cc, wc):
    nb, seq, d = x.shape
    m = nb * TQ
    w_in, w_glu, w_a_out, w_b_out, w_o = dense
    s5w = w_glu.shape[1]
    n_half, half_w, _ = bc.shape
    half_st = half_w // S5_GROUP_CH * cc.shape[2]
    nst = n_half * half_st
    lw = w_b_out.shape[1]
    stage_cols = max(w.shape[2] for w in dense)
    in_specs = [pl.BlockSpec((nb, TQ, d), lambda i: (0, i, 0)),
                _const_spec(vec.array.shape)]
    in_specs += [pl.BlockSpec(memory_space=pl.ANY) for _ in dense]
    in_specs += [_const_spec(c.shape) for c in (bc, cc, wc)]
    scratch = [pltpu.VMEM(w.shape[1:], BF16) for w in dense]
    scratch += [
        pltpu.VMEM((STAGE_SLOTS, MIXER_STAGE_ROWS, stage_cols), F32),
        pltpu.SemaphoreType.DMA((STAGE_SLOTS,)),
        pltpu.VMEM((n_half, half_w, 2 * half_st), BF16),
        pltpu.VMEM((n_half, 2, half_st, half_w), BF16),
        pltpu.VMEM((wc.shape[0], MXU_DIM, 2 * MXU_DIM), BF16),
        pltpu.VMEM((m, d), F32),
        pltpu.VMEM((m, 2 * d), BF16),
        pltpu.VMEM((m, s5w), F32),
        pltpu.VMEM((m + (CONV_WIDTH - 1) * nb, lw), F32),
        pltpu.VMEM((m, nst), F32),
        pltpu.VMEM((m, nst), F32),
        pltpu.VMEM((2, nb, nst), F32),
        pltpu.VMEM((m, lw), F32),
        pltpu.VMEM((m, lw), F32),
        pltpu.VMEM((nb, lw), F32),
    ]
    return pl.pallas_call(
        functools.partial(_mixer_kernel, vec),
        out_shape=jax.ShapeDtypeStruct(x.shape, F32),
        grid=(seq // TQ,),
        in_specs=in_specs,
        out_specs=pl.BlockSpec((nb, TQ, d), lambda i: (0, i, 0)),
        scratch_shapes=scratch,
        compiler_params=pltpu.CompilerParams(
            dimension_semantics=("arbitrary",), vmem_limit_bytes=VMEM_LIMIT),
        name="mixer",
    )(x, vec.array, *dense, bc, cc, wc)


def _ffn_kernel(vec, x_ref, p_ref, vec_ref, wgate_hbm, wup_hbm, wdown_hbm,
                wpg_hbm, wple_hbm, o_ref,
                wgate, wup, wdown, wpg, wple, stage, sem):
    v = vec.view(vec_ref)

    @pl.when(pl.program_id(0) == 0)
    def _():
        _stream_cast([(wgate_hbm.at[0], wgate), (wup_hbm.at[0], wup),
                      (wdown_hbm.at[0], wdown), (wpg_hbm.at[0], wpg),
                      (wple_hbm.at[0], wple)], stage, sem)

    hid = wgate.shape[1]
    for s in range(FFN_ROWS // FFN_SUB_ROWS):
        rows = pl.ds(s * FFN_SUB_ROWS, FFN_SUB_ROWS)
        x = x_ref[rows, :]
        h2 = _rms(x, v("g_ffn")).astype(BF16)
        acc = x
        for j in range(hid // FFN_HID_BLK):
            cs = slice(j * FFN_HID_BLK, (j + 1) * FFN_HID_BLK)
            gt = _dot(h2, wgate[:, cs])
            a = gt * _sigmoid(gt) * _dot(h2, wup[:, cs])
            acc = acc + _dot(a.astype(BF16), wdown[cs, :])
        x2 = acc
        gate_p = _sigmoid(
            _dot(_rms(x2, v("g_ple_gate")).astype(BF16), wpg[...]) + v("b_ple_gate"))
        e = _rms(_dot(p_ref[rows, :].astype(BF16), wple[...]), v("g_ple"))
        x3 = x2 + gate_p * e
        o_ref[rows, :] = _rms(x3, v("g_final"))


def _ffn_call(x1, p, vec, dense):
    bsz, seq, d = x1.shape
    pd = p.shape[-1]
    per_seq = seq // FFN_ROWS
    stage_cols = max(w.shape[2] for w in dense)
    in_specs = [
        pl.BlockSpec((None, FFN_ROWS, d), lambda i: (i // per_seq, i % per_seq, 0)),
        pl.BlockSpec((None, None, FFN_ROWS, pd),
                     lambda i: (0, i // per_seq, i % per_seq, 0)),
        _const_spec(vec.array.shape)]
    in_specs += [pl.BlockSpec(memory_space=pl.ANY) for _ in dense]
    scratch = [pltpu.VMEM(w.shape[1:], BF16) for w in dense]
    scratch += [pltpu.VMEM((STAGE_SLOTS, FFN_STAGE_ROWS, stage_cols), F32),
                pltpu.SemaphoreType.DMA((STAGE_SLOTS,))]
    return pl.pallas_call(
        functools.partial(_ffn_kernel, vec),
        out_shape=jax.ShapeDtypeStruct((bsz, seq, d), F32),
        grid=(bsz * per_seq,),
        in_specs=in_specs,
        out_specs=pl.BlockSpec((None, FFN_ROWS, d),
                               lambda i: (i // per_seq, i % per_seq, 0)),
        scratch_shapes=scratch,
        compiler_params=pltpu.CompilerParams(
            dimension_semantics=("arbitrary",), vmem_limit_bytes=VMEM_LIMIT),
        name="ffn",
    )(x1, p, vec.array, *dense)


def _s5_params(lam_re, lam_im, log_dt, b_re, b_im, c_re, c_im):
    g, n = lam_re.shape
    p = b_re.shape[2]
    hg = g // 2
    dt = jnp.exp(log_dt)[:, None]
    mag = jnp.exp(lam_re * dt)
    ar = mag * jnp.cos(lam_im * dt)
    ai = mag * jnp.sin(lam_im * dt)
    den = lam_re * lam_re + lam_im * lam_im
    nr = ar - 1.0
    fr = (nr * lam_re + ai * lam_im) / den
    fi = (ai * lam_re - nr * lam_im) / den
    bbr = fr[..., None] * b_re - fi[..., None] * b_im
    bbi = fr[..., None] * b_im + fi[..., None] * b_re
    bc = jnp.stack([bbr, bbi]).reshape(2, 2, hg, n, p).transpose(1, 2, 4, 0, 3)
    bc = bc.reshape(2, hg * p, 2 * n)
    cc = jnp.stack([c_re, -c_im]).reshape(2, 2, hg, p, n).transpose(1, 0, 4, 2, 3)
    cc = cc.reshape(2, 2, n, hg * p)
    return ar.reshape(-1), ai.reshape(-1), bc, cc


def _lru_gate_weights(w_r, w_i):
    heads, hd, _ = w_r.shape
    per = MXU_DIM // hd
    tiles = heads // per
    w = jnp.stack([w_r, w_i]).reshape(2, tiles, per, hd, hd).transpose(1, 2, 3, 0, 4)
    return w.reshape(tiles, per * hd, 2 * hd)


def kernel(x, p, g_mix, w_in, b_in, lam_re, lam_im, log_dt, s5_b_re, s5_b_im, s5_c_re, s5_c_im, s5_d, w_glu, b_glu, conv_w, conv_b, w_r, b_r, w_i, b_i, lru_lambda, w_a_out, w_b_out, w_o, g_ffn, w_ffn_gate, w_ffn_up, w_ffn_down, g_ple_gate, w_ple_gate, b_ple_gate, w_ple, g_ple, g_final):
    bsz, seq, d = x.shape
    assert w_in.shape[0] == 1
    assert bsz == SUBLANES and seq % TQ == 0 and seq % FFN_ROWS == 0

    a_re, a_im, bc, cc = _s5_params(
        lam_re[0], lam_im[0], log_dt[0], s5_b_re[0], s5_b_im[0],
        s5_c_re[0], s5_c_im[0])
    mixer_vec = _Packed([
        ("g_mix", g_mix), ("b_in", b_in), ("a_re", a_re), ("a_im", a_im),
        ("s5_d", s5_d), ("b_glu", b_glu), ("conv_w", conv_w), ("conv_b", conv_b),
        ("b_r", b_r), ("b_i", b_i), ("lru_lambda", lru_lambda)])
    x1 = _mixer_call(x, mixer_vec, (w_in, w_glu, w_a_out, w_b_out, w_o),
                     bc, cc, _lru_gate_weights(w_r[0], w_i[0]))
    ffn_vec = _Packed([
        ("g_ffn", g_ffn), ("g_ple_gate", g_ple_gate), ("b_ple_gate", b_ple_gate),
        ("g_ple", g_ple), ("g_final", g_final)])
    return _ffn_call(x1, p, ffn_vec,
                     (w_ffn_gate, w_ffn_up, w_ffn_down, w_ple_gate, w_ple))
```

```python
import functools

import jax
import jax.numpy as jnp
from jax import lax
from jax.experimental import pallas as pl
from jax.experimental.pallas import tpu as pltpu

F32 = jnp.float32
BF16 = jnp.bfloat16

EPS = 1e-6
LRU_C = 8.0
CONV_WIDTH = 4
LRU_HEAD_DIM = 64
S5_GROUP_CH = 16

SUBLANES = 8
LANES = 128
MXU_DIM = 256
TQ = 64
SCAN_LANES = 512
SCAN_UNROLL = 64
STAGE_SLOTS = 4
MIXER_STAGE_ROWS = 64
FFN_ROWS = 1024
FFN_SUB_ROWS = 512
FFN_HID_BLK = 256
FFN_STAGE_ROWS = 128
VMEM_LIMIT = 56 * 1024 * 1024


def _rms(x, g):
    return x * lax.rsqrt(jnp.mean(x * x, axis=-1, keepdims=True) + EPS) * g


def _dot(a, b):
    return jnp.dot(a, b, preferred_element_type=F32)


def _sigmoid_of_twice(half_x):
    return 0.5 * jnp.tanh(half_x) + 0.5


def _sigmoid(x):
    return _sigmoid_of_twice(0.5 * x)


def _sqrt_nonneg(v):
    return jnp.where(v > 0.0, v * lax.rsqrt(v), 0.0)


def _softplus(x):
    return jnp.maximum(x, 0.0) + jnp.log1p(jnp.exp(-jnp.abs(x)))


class _Packed:
    def __init__(self, named):
        self.offsets = {}
        parts, off = [], 0
        for name, v in named:
            v = v.reshape(-1).astype(F32)
            assert v.shape[0] % LANES == 0
            self.offsets[name] = (off, v.shape[0])
            parts.append(v)
            off += v.shape[0]
        self.array = jnp.concatenate(parts).reshape(1, off)

    def view(self, ref):
        offsets = self.offsets

        def get(name, lo=0, width=None):
            off, n = offsets[name]
            width = n - lo if width is None else width
            return ref[:, off + lo:off + lo + width]
        return get


def _stream_cast(jobs, stage, sem):
    n_slots, stage_rows, stage_cols = stage.shape
    chunks = []
    for src, dst, scale in jobs:
        rows, cols = dst.shape
        rc = min(rows, stage_rows)
        assert rows % rc == 0 and cols <= stage_cols
        for k in range(rows // rc):
            chunks.append((src, dst, k * rc, rc, cols, scale))

    def copy(i):
        src, _, r0, rc, cols, _ = chunks[i]
        slot = i % n_slots
        return pltpu.make_async_copy(
            src.at[pl.ds(r0, rc), :],
            stage.at[slot, pl.ds(0, rc), pl.ds(0, cols)],
            sem.at[slot])

    ahead = n_slots - 1
    for i in range(min(ahead, len(chunks))):
        copy(i).start()
    for i in range(len(chunks)):
        if i + ahead < len(chunks):
            copy(i + ahead).start()
        copy(i).wait()
        _, dst, r0, rc, cols, scale = chunks[i]
        w = stage[i % n_slots, pl.ds(0, rc), pl.ds(0, cols)]
        if scale is not None:
            w = w * scale
        dst[pl.ds(r0, rc), :] = w.astype(BF16)


def _pow2_div(x, n):
    assert n & (n - 1) == 0
    return x >> (n.bit_length() - 1)


def _pow2_mod(x, n):
    assert n & (n - 1) == 0
    return x & (n - 1)


def _block_diag_cols(compact, row_blk, col_blk, n_blocks):
    rows, ccols = compact.shape
    n_parts = ccols // col_blk
    part_w = n_blocks * col_blk
    ocols = n_parts * part_w
    k_i = lax.broadcasted_iota(jnp.int32, (ccols, ocols), 0)
    c_i = lax.broadcasted_iota(jnp.int32, (ccols, ocols), 1)
    onehot = ((_pow2_div(k_i, col_blk) == _pow2_div(c_i, part_w))
              & (_pow2_mod(k_i, col_blk) == _pow2_mod(c_i, col_blk)))
    tiled = _dot(compact.astype(BF16), onehot.astype(BF16))
    r_o = lax.broadcasted_iota(jnp.int32, (rows, ocols), 0)
    c_o = lax.broadcasted_iota(jnp.int32, (rows, ocols), 1)
    keep = _pow2_div(r_o, row_blk) == _pow2_div(_pow2_mod(c_o, part_w), col_blk)
    return jnp.where(keep, tiled, 0.0).astype(BF16)


def _block_diag_rows(compact, row_blk, col_blk, n_blocks):
    tiled = jnp.concatenate([compact] * n_blocks, axis=0)
    r_o = lax.broadcasted_iota(jnp.int32, tiled.shape, 0)
    c_o = lax.broadcasted_iota(jnp.int32, tiled.shape, 1)
    keep = _pow2_div(r_o, row_blk) == _pow2_div(c_o, col_blk)
    return jnp.where(keep, tiled, 0.0).astype(BF16)


def _mixer_kernel(vec, x_ref, vec_ref, win_hbm, wglu_hbm, waout_hbm,
                  wbout_hbm, wo_hbm, bc_ref, cc_ref, wc_ref, o_ref,
                  win, wglu, waout, wbout, wo, stage, sem, bbd_ref, ccat_ref, wg_ref,
                  gates, ua, ubuf, xr, xi, s5st, abuf, bxbuf, lrust):
    nb, tq, d = x_ref.shape
    m = nb * tq
    s5w = ua.shape[1]
    nst = xr.shape[1]
    half_w = s5w // 2
    half_st = nst // 2
    lw = abuf.shape[1]
    hdr = (CONV_WIDTH - 1) * nb
    v = vec.view(vec_ref)

    @pl.when(pl.program_id(0) == 0)
    def _():
        col = lax.broadcasted_iota(jnp.int32, (1, win.shape[1]), 1)
        in_scale = jnp.where(col >= s5w + lw, 0.5, 1.0)
        glu_scale = jnp.full((1, wglu.shape[1]), 0.5, F32)
        _stream_cast([(win_hbm.at[0], win, in_scale),
                      (wglu_hbm.at[0], wglu, glu_scale),
                      (waout_hbm.at[0], waout, None), (wbout_hbm.at[0], wbout, None),
                      (wo_hbm.at[0], wo, None)], stage, sem)
        s5_p = S5_GROUP_CH
        s5_n = cc_ref.shape[2]
        n_grp = bc_ref.shape[1] // s5_p
        for hh in range(bbd_ref.shape[0]):
            bbd_ref[hh] = _block_diag_cols(bc_ref[hh], s5_p, s5_n, n_grp)
            for ri in range(2):
                ccat_ref[hh, ri] = _block_diag_rows(cc_ref[hh, ri], s5_n, s5_p, n_grp)
        for j in range(wg_ref.shape[0]):
            wg_ref[j] = _block_diag_cols(0.5 * wc_ref[j], LRU_HEAD_DIM, LRU_HEAD_DIM,
                                         MXU_DIM // LRU_HEAD_DIM)
        s5st[...] = jnp.zeros_like(s5st)
        lrust[...] = jnp.zeros_like(lrust)
        ubuf[pl.ds(0, hdr), :] = jnp.zeros((hdr, lw), F32)

    xt = jnp.swapaxes(x_ref[...], 0, 1).reshape(m, d)

    neg_sp = -LRU_C * _softplus(-v("lru_lambda"))
    h = _rms(xt, v("g_mix")).astype(BF16)
    g0 = s5w + lw
    n_blk = lw // MXU_DIM
    gate_w = 2 * d // n_blk

    def in_proj(lo, width, bias_scale=1.0):
        return _dot(h, win[:, lo:lo + width]) + bias_scale * v("b_in", lo, width)

    u = in_proj(0, s5w)
    ua[...] = u
    ub16 = u.astype(BF16)
    def lru_block(j):
        cs = slice(j * MXU_DIM, (j + 1) * MXU_DIM)
        ub = in_proj(s5w + j * MXU_DIM, MXU_DIM)
        ubuf[pl.ds(hdr, m), cs] = ub
        xc = (v("conv_b", j * MXU_DIM, MXU_DIM)
              + v("conv_w", (CONV_WIDTH - 1) * lw + j * MXU_DIM, MXU_DIM) * ub)
        for k in range(CONV_WIDTH - 1):
            xc = xc + (v("conv_w", k * lw + j * MXU_DIM, MXU_DIM)
                       * ubuf[pl.ds(k * nb, m), cs])
        if j < 2:
            pr = _dot(ub16[:, j * half_w:(j + 1) * half_w], bbd_ref[j])
            xr[:, j * half_st:(j + 1) * half_st] = pr[:, :half_st]
            xi[:, j * half_st:(j + 1) * half_st] = pr[:, half_st:]
        g = _dot(xc.astype(BF16), wg_ref[j])
        r = _sigmoid_of_twice(g[:, :MXU_DIM] + 0.5 * v("b_r", j * MXU_DIM, MXU_DIM))
        ig = _sigmoid_of_twice(g[:, MXU_DIM:] + 0.5 * v("b_i", j * MXU_DIM, MXU_DIM))
        a = jnp.exp(r * neg_sp[:, cs])
        abuf[:, cs] = a
        bxbuf[:, cs] = _sqrt_nonneg(1.0 - a * a) * ig * xc
        gates[:, j * gate_w:(j + 1) * gate_w] = _sigmoid_of_twice(
            in_proj(g0 + j * gate_w, gate_w, bias_scale=0.5)).astype(BF16)
    n_pass = nst // SCAN_LANES
    assert n_pass == n_blk and SCAN_LANES * 2 == half_st

    def scan_pass(c):
        ls = slice(c * SCAN_LANES, (c + 1) * SCAN_LANES)
        ll = slice(c * MXU_DIM, (c + 1) * MXU_DIM)
        ar = jnp.broadcast_to(v("a_re", c * SCAN_LANES, SCAN_LANES), (nb, SCAN_LANES))
        ai = jnp.broadcast_to(v("a_im", c * SCAN_LANES, SCAN_LANES), (nb, SCAN_LANES))

        def scan_step(t, carry):
            sr, si, hl = carry
            rows = pl.ds(pl.multiple_of(t * nb, nb), nb)
            nsr = ar * sr - ai * si + xr[rows, ls]
            nsi = ar * si + ai * sr + xi[rows, ls]
            xr[rows, ls] = nsr
            xi[rows, ls] = nsi
            hn = abuf[rows, ll] * hl + bxbuf[rows, ll]
            bxbuf[rows, ll] = hn
            return nsr, nsi, hn

        sr, si, hl = lax.fori_loop(
            0, tq, scan_step, (s5st[0, :, ls], s5st[1, :, ls], lrust[:, ll]),
            unroll=SCAN_UNROLL)
        s5st[0, :, ls] = sr
        s5st[1, :, ls] = si
        lrust[:, ll] = hl

    def s5_readout(hh):
        ls = slice(hh * half_st, (hh + 1) * half_st)
        return (_dot(xr[:, ls].astype(BF16), ccat_ref[hh, 0])
                + _dot(xi[:, ls].astype(BF16), ccat_ref[hh, 1]))

    lru_block(0)
    lru_block(1)
    lru_block(2)
    scan_pass(0)
    lru_block(3)
    ubuf[pl.ds(0, hdr), :] = ubuf[pl.ds(m, hdr), :]
    scan_pass(1)
    y0 = s5_readout(0)
    scan_pass(2)
    scan_pass(3)
    y1 = s5_readout(1)

    y = jnp.concatenate([y0, y1], axis=1) + v("s5_d") * ua[...]
    z = jax.nn.gelu(y)
    y_a = z * _sigmoid_of_twice(_dot(z.astype(BF16), wglu[...]) + 0.5 * v("b_glu"))
    merged = gates[:, 0:d] * _dot(y_a.astype(BF16), waout[...])
    merged = merged + gates[:, d:2 * d] * _dot(bxbuf[...].astype(BF16), wbout[...])
    delta = _dot(merged.astype(BF16), wo[...])
    o_ref[...] = jnp.swapaxes(delta.reshape(tq, nb, d), 0, 1) + x_ref[...]


def _const_spec(shape):
    nd = len(shape)
    return pl.BlockSpec(shape, lambda i, _nd=nd: (0,) * _nd,
                        pipeline_mode=pl.Buffered(1))


def _mixer_call(x, vec, dense, bc, cc, wc):
    nb, seq, d = x.shape
    m = nb * TQ
    w_in, w_glu, w_a_out, w_b_out, w_o = dense
    s5w = w_glu.shape[1]
    n_half, half_w, _ = bc.shape
    half_st = half_w // S5_GROUP_CH * cc.shape[2]
    nst = n_half * half_st
    lw = w_b_out.shape[1]
    stage_cols = max(w.shape[2] for w in dense)
    tile_spec = pl.BlockSpec((nb, TQ, d), lambda i: (0, i, 0))
    in_specs = [tile_spec, _const_spec(vec.array.shape)]
    in_specs += [pl.BlockSpec(memory_space=pl.ANY) for _ in dense]
    in_specs += [_const_spec(c.shape) for c in (bc, cc, wc)]
    scratch = [pltpu.VMEM(w.shape[1:], BF16) for w in dense]
    scratch += [
        pltpu.VMEM((STAGE_SLOTS, MIXER_STAGE_ROWS, stage_cols), F32),
        pltpu.SemaphoreType.DMA((STAGE_SLOTS,)),
        pltpu.VMEM((n_half, half_w, 2 * half_st), BF16),
        pltpu.VMEM((n_half, 2, half_st, half_w), BF16),
        pltpu.VMEM((wc.shape[0], MXU_DIM, 2 * MXU_DIM), BF16),
        pltpu.VMEM((m, 2 * d), BF16),
        pltpu.VMEM((m, s5w), F32),
        pltpu.VMEM((m + (CONV_WIDTH - 1) * nb, lw), F32),
        pltpu.VMEM((m, nst), F32),
        pltpu.VMEM((m, nst), F32),
        pltpu.VMEM((2, nb, nst), F32),
        pltpu.VMEM((m, lw), F32),
        pltpu.VMEM((m, lw), F32),
        pltpu.VMEM((nb, lw), F32),
    ]
    return pl.pallas_call(
        functools.partial(_mixer_kernel, vec),
        out_shape=jax.ShapeDtypeStruct(x.shape, F32),
        grid=(seq // TQ,),
        in_specs=in_specs,
        out_specs=tile_spec,
        scratch_shapes=scratch,
        compiler_params=pltpu.CompilerParams(
            dimension_semantics=("arbitrary",), vmem_limit_bytes=VMEM_LIMIT),
        name="mixer",
    )(x, vec.array, *dense, bc, cc, wc)


def _ffn_kernel(vec, x_ref, p_ref, vec_ref, wgate_hbm, wup_hbm, wdown_hbm,
                wpg_hbm, wple_hbm, o_ref,
                wgate, wup, wdown, wpg, wple, stage, sem):
    v = vec.view(vec_ref)

    @pl.when(pl.program_id(0) == 0)
    def _():
        _stream_cast([(wgate_hbm.at[0], wgate, None), (wup_hbm.at[0], wup, None),
                      (wdown_hbm.at[0], wdown, None), (wpg_hbm.at[0], wpg, None),
                      (wple_hbm.at[0], wple, None)], stage, sem)

    hid = wgate.shape[1]
    for s in range(FFN_ROWS // FFN_SUB_ROWS):
        rows = pl.ds(s * FFN_SUB_ROWS, FFN_SUB_ROWS)
        x = x_ref[rows, :]
        h2 = _rms(x, v("g_ffn")).astype(BF16)
        acc = x
        for j in range(hid // FFN_HID_BLK):
            cs = slice(j * FFN_HID_BLK, (j + 1) * FFN_HID_BLK)
            gt = _dot(h2, wgate[:, cs])
            a = gt * _sigmoid(gt) * _dot(h2, wup[:, cs])
            acc = acc + _dot(a.astype(BF16), wdown[cs, :])
        x2 = acc
        gate_p = _sigmoid(
            _dot(_rms(x2, v("g_ple_gate")).astype(BF16), wpg[...]) + v("b_ple_gate"))
        e = _rms(_dot(p_ref[rows, :].astype(BF16), wple[...]), v("g_ple"))
        x3 = x2 + gate_p * e
        o_ref[rows, :] = _rms(x3, v("g_final"))


def _ffn_call(x1, p, vec, dense):
    bsz, seq, d = x1.shape
    pd = p.shape[-1]
    per_seq = seq // FFN_ROWS
    stage_cols = max(w.shape[2] for w in dense)
    in_specs = [
        pl.BlockSpec((None, FFN_ROWS, d), lambda i: (i // per_seq, i % per_seq, 0)),
        pl.BlockSpec((None, None, FFN_ROWS, pd),
                     lambda i: (0, i // per_seq, i % per_seq, 0)),
        _const_spec(vec.array.shape)]
    in_specs += [pl.BlockSpec(memory_space=pl.ANY) for _ in dense]
    scratch = [pltpu.VMEM(w.shape[1:], BF16) for w in dense]
    scratch += [pltpu.VMEM((STAGE_SLOTS, FFN_STAGE_ROWS, stage_cols), F32),
                pltpu.SemaphoreType.DMA((STAGE_SLOTS,))]
    return pl.pallas_call(
        functools.partial(_ffn_kernel, vec),
        out_shape=jax.ShapeDtypeStruct((bsz, seq, d), F32),
        grid=(bsz * per_seq,),
        in_specs=in_specs,
        out_specs=pl.BlockSpec((None, FFN_ROWS, d),
                               lambda i: (i // per_seq, i % per_seq, 0)),
        scratch_shapes=scratch,
        compiler_params=pltpu.CompilerParams(
            dimension_semantics=("arbitrary",), vmem_limit_bytes=VMEM_LIMIT),
        name="ffn",
    )(x1, p, vec.array, *dense)


def _s5_params(lam_re, lam_im, log_dt, b_re, b_im, c_re, c_im):
    g, n = lam_re.shape
    p = b_re.shape[2]
    hg = g // 2
    dt = jnp.exp(log_dt)[:, None]
    mag = jnp.exp(lam_re * dt)
    ar = mag * jnp.cos(lam_im * dt)
    ai = mag * jnp.sin(lam_im * dt)
    den = lam_re * lam_re + lam_im * lam_im
    nr = ar - 1.0
    fr = (nr * lam_re + ai * lam_im) / den
    fi = (ai * lam_re - nr * lam_im) / den
    bbr = fr[..., None] * b_re - fi[..., None] * b_im
    bbi = fr[..., None] * b_im + fi[..., None] * b_re
    bc = jnp.stack([bbr, bbi]).reshape(2, 2, hg, n, p).transpose(1, 2, 4, 0, 3)
    bc = bc.reshape(2, hg * p, 2 * n)
    cc = jnp.stack([c_re, -c_im]).reshape(2, 2, hg, p, n).transpose(1, 0, 4, 2, 3)
    cc = cc.reshape(2, 2, n, hg * p)
    return ar.reshape(-1), ai.reshape(-1), bc, cc


def _lru_gate_weights(w_r, w_i):
    heads, hd, _ = w_r.shape
    per = MXU_DIM // hd
    tiles = heads // per
    w = jnp.stack([w_r, w_i]).reshape(2, tiles, per, hd, hd).transpose(1, 2, 3, 0, 4)
    return w.reshape(tiles, per * hd, 2 * hd)


def kernel(x, p, g_mix, w_in, b_in, lam_re, lam_im, log_dt, s5_b_re, s5_b_im, s5_c_re, s5_c_im, s5_d, w_glu, b_glu, conv_w, conv_b, w_r, b_r, w_i, b_i, lru_lambda, w_a_out, w_b_out, w_o, g_ffn, w_ffn_gate, w_ffn_up, w_ffn_down, g_ple_gate, w_ple_gate, b_ple_gate, w_ple, g_ple, g_final):
    bsz, seq, d = x.shape
    assert w_in.shape[0] == 1
    assert bsz == SUBLANES and seq % TQ == 0 and seq % FFN_ROWS == 0

    a_re, a_im, bc, cc = _s5_params(
        lam_re[0], lam_im[0], log_dt[0], s5_b_re[0], s5_b_im[0],
        s5_c_re[0], s5_c_im[0])
    mixer_vec = _Packed([
        ("g_mix", g_mix), ("b_in", b_in), ("a_re", a_re), ("a_im", a_im),
        ("s5_d", s5_d), ("b_glu", b_glu), ("conv_w", conv_w), ("conv_b", conv_b),
        ("b_r", b_r), ("b_i", b_i), ("lru_lambda", lru_lambda)])
    x1 = _mixer_call(x, mixer_vec, (w_in, w_glu, w_a_out, w_b_out, w_o),
                     bc, cc, _lru_gate_weights(w_r[0], w_i[0]))
    ffn_vec = _Packed([
        ("g_ffn", g_ffn), ("g_ple_gate", g_ple_gate), ("b_ple_gate", b_ple_gate),
        ("g_ple", g_ple), ("g_final", g_final)])
    return _ffn_call(x1, p, ffn_vec,
                     (w_ffn_gate, w_ffn_up, w_ffn_down, w_ple_gate, w_ple))
```

```python
import functools

import jax
import jax.numpy as jnp
from jax import lax
from jax.experimental import pallas as pl
from jax.experimental.pallas import tpu as pltpu

F32 = jnp.float32
BF16 = jnp.bfloat16

EPS = 1e-6
LRU_C = 8.0
CONV_WIDTH = 4
LRU_HEAD_DIM = 64
S5_GROUP_CH = 16

SUBLANES = 8
LANES = 128
MXU_DIM = 256
TQ = 64
SCAN_LANES = 512
SCAN_UNROLL = 64
STAGE_SLOTS = 4
MIXER_STAGE = (256, 1024)
FFN_ROWS = 1024
FFN_SUB_ROWS = 512
FFN_HID_BLK = 256
FFN_STAGE = (256, 1408)
VMEM_LIMIT = 56 * 1024 * 1024


def _rms(x, g):
    return x * lax.rsqrt(jnp.mean(x * x, axis=-1, keepdims=True) + EPS) * g


def _dot(a, b):
    return jnp.dot(a, b, preferred_element_type=F32)


def _sigmoid_of_twice(half_x):
    return 0.5 * jnp.tanh(half_x) + 0.5


def _sigmoid(x):
    return _sigmoid_of_twice(0.5 * x)


def _sqrt_nonneg(v):
    return jnp.where(v > 0.0, v * lax.rsqrt(v), 0.0)


def _softplus(x):
    return jnp.maximum(x, 0.0) + jnp.log1p(jnp.exp(-jnp.abs(x)))


class _Packed:
    def __init__(self, named):
        self.offsets = {}
        parts, off = [], 0
        for name, v in named:
            v = v.reshape(-1).astype(F32)
            assert v.shape[0] % LANES == 0
            self.offsets[name] = (off, v.shape[0])
            parts.append(v)
            off += v.shape[0]
        self.array = jnp.concatenate(parts).reshape(1, off)

    def view(self, ref):
        offsets = self.offsets

        def get(name, lo=0, width=None):
            off, n = offsets[name]
            width = n - lo if width is None else width
            return ref[:, off + lo:off + lo + width]
        return get


def _stream_cast(jobs, stage, sem):
    n_slots, stage_rows, stage_cols = stage.shape
    chunks = []
    for src, dst, scale in jobs:
        rows, cols = dst.shape
        rc = min(rows, stage_rows)
        assert rows % rc == 0 and cols % LANES == 0
        for r0 in range(0, rows, rc):
            for c0 in range(0, cols, stage_cols):
                chunks.append((src, dst, r0, rc, c0, min(stage_cols, cols - c0), scale))

    def copy(i):
        src, _, r0, rc, c0, cw, _ = chunks[i]
        slot = i % n_slots
        return pltpu.make_async_copy(
            src.at[pl.ds(r0, rc), pl.ds(c0, cw)],
            stage.at[slot, pl.ds(0, rc), pl.ds(0, cw)],
            sem.at[slot])

    ahead = n_slots - 1
    for i in range(min(ahead, len(chunks))):
        copy(i).start()
    for i in range(len(chunks)):
        if i + ahead < len(chunks):
            copy(i + ahead).start()
        copy(i).wait()
        _, dst, r0, rc, c0, cw, scale = chunks[i]
        w = stage[i % n_slots, pl.ds(0, rc), pl.ds(0, cw)]
        if scale is not None:
            w = w * scale[:, c0:c0 + cw]
        dst[pl.ds(r0, rc), pl.ds(c0, cw)] = w.astype(BF16)


def _pow2_div(x, n):
    assert n & (n - 1) == 0
    return x >> (n.bit_length() - 1)


def _pow2_mod(x, n):
    assert n & (n - 1) == 0
    return x & (n - 1)


def _block_diag_cols(compact, row_blk, col_blk, n_blocks):
    rows, ccols = compact.shape
    n_parts = ccols // col_blk
    part_w = n_blocks * col_blk
    ocols = n_parts * part_w
    k_i = lax.broadcasted_iota(jnp.int32, (ccols, ocols), 0)
    c_i = lax.broadcasted_iota(jnp.int32, (ccols, ocols), 1)
    onehot = ((_pow2_div(k_i, col_blk) == _pow2_div(c_i, part_w))
              & (_pow2_mod(k_i, col_blk) == _pow2_mod(c_i, col_blk)))
    tiled = _dot(compact.astype(BF16), onehot.astype(BF16))
    r_o = lax.broadcasted_iota(jnp.int32, (rows, ocols), 0)
    c_o = lax.broadcasted_iota(jnp.int32, (rows, ocols), 1)
    keep = _pow2_div(r_o, row_blk) == _pow2_div(_pow2_mod(c_o, part_w), col_blk)
    return jnp.where(keep, tiled, 0.0).astype(BF16)


def _block_diag_rows(compact, row_blk, col_blk, n_blocks):
    tiled = jnp.concatenate([compact] * n_blocks, axis=0)
    r_o = lax.broadcasted_iota(jnp.int32, tiled.shape, 0)
    c_o = lax.broadcasted_iota(jnp.int32, tiled.shape, 1)
    keep = _pow2_div(r_o, row_blk) == _pow2_div(c_o, col_blk)
    return jnp.where(keep, tiled, 0.0).astype(BF16)


def _mixer_kernel(vec, x_ref, vec_ref, win_hbm, wglu_hbm, waout_hbm,
                  wbout_hbm, wo_hbm, bc_ref, cc_ref, wc_ref, o_ref,
                  win, wglu, waout, wbout, wo, stage, sem, bbd_ref, ccat_ref, wg_ref,
                  gates, ua, ubuf, xr, xi, s5st, abuf, bxbuf, lrust):
    nb, tq, d = x_ref.shape
    m = nb * tq
    s5w = ua.shape[1]
    nst = xr.shape[1]
    half_w = s5w // 2
    half_st = nst // 2
    lw = abuf.shape[1]
    hdr = (CONV_WIDTH - 1) * nb
    v = vec.view(vec_ref)

    def time_major_norm(tile_ref):
        xt = jnp.swapaxes(tile_ref[...], 0, 1).reshape(m, d)
        return _rms(xt, v("g_mix")).astype(BF16)

    @pl.when(pl.program_id(0) == 0)
    def _():
        col = lax.broadcasted_iota(jnp.int32, (1, win.shape[1]), 1)
        in_scale = jnp.where(col >= s5w + lw, 0.5, 1.0)
        glu_scale = jnp.full((1, wglu.shape[1]), 0.5, F32)
        _stream_cast([(win_hbm.at[0], win, in_scale),
                      (wglu_hbm.at[0], wglu, glu_scale),
                      (waout_hbm.at[0], waout, None), (wbout_hbm.at[0], wbout, None),
                      (wo_hbm.at[0], wo, None)], stage, sem)
        s5_p = S5_GROUP_CH
        s5_n = cc_ref.shape[2]
        n_grp = bc_ref.shape[1] // s5_p
        for hh in range(bbd_ref.shape[0]):
            bbd_ref[hh] = _block_diag_cols(bc_ref[hh], s5_p, s5_n, n_grp)
            for ri in range(2):
                ccat_ref[hh, ri] = _block_diag_rows(cc_ref[hh, ri], s5_n, s5_p, n_grp)
        for j in range(wg_ref.shape[0]):
            wg_ref[j] = _block_diag_cols(0.5 * wc_ref[j], LRU_HEAD_DIM, LRU_HEAD_DIM,
                                         MXU_DIM // LRU_HEAD_DIM)
        s5st[...] = jnp.zeros_like(s5st)
        lrust[...] = jnp.zeros_like(lrust)
        ubuf[pl.ds(0, hdr), :] = jnp.zeros((hdr, lw), F32)

    h = time_major_norm(x_ref)
    neg_sp = -LRU_C * _softplus(-v("lru_lambda"))
    g0 = s5w + lw
    n_blk = lw // MXU_DIM
    gate_w = 2 * d // n_blk

    def in_proj(lo, width, bias_scale=1.0):
        return _dot(h, win[:, lo:lo + width]) + bias_scale * v("b_in", lo, width)

    u = in_proj(0, s5w)
    ua[...] = u
    ub16 = u.astype(BF16)
    def lru_block(j):
        cs = slice(j * MXU_DIM, (j + 1) * MXU_DIM)
        ub = in_proj(s5w + j * MXU_DIM, MXU_DIM)
        ubuf[pl.ds(hdr, m), cs] = ub
        xc = (v("conv_b", j * MXU_DIM, MXU_DIM)
              + v("conv_w", (CONV_WIDTH - 1) * lw + j * MXU_DIM, MXU_DIM) * ub)
        for k in range(CONV_WIDTH - 1):
            xc = xc + (v("conv_w", k * lw + j * MXU_DIM, MXU_DIM)
                       * ubuf[pl.ds(k * nb, m), cs])
        if j < 2:
            pr = _dot(ub16[:, j * half_w:(j + 1) * half_w], bbd_ref[j])
            xr[:, j * half_st:(j + 1) * half_st] = pr[:, :half_st]
            xi[:, j * half_st:(j + 1) * half_st] = pr[:, half_st:]
        g = _dot(xc.astype(BF16), wg_ref[j])
        r = _sigmoid_of_twice(g[:, :MXU_DIM] + 0.5 * v("b_r", j * MXU_DIM, MXU_DIM))
        ig = _sigmoid_of_twice(g[:, MXU_DIM:] + 0.5 * v("b_i", j * MXU_DIM, MXU_DIM))
        a = jnp.exp(r * neg_sp[:, cs])
        abuf[:, cs] = a
        bxbuf[:, cs] = _sqrt_nonneg(1.0 - a * a) * ig * xc
        gates[:, j * gate_w:(j + 1) * gate_w] = _sigmoid_of_twice(
            in_proj(g0 + j * gate_w, gate_w, bias_scale=0.5)).astype(BF16)
    n_pass = nst // SCAN_LANES
    lru_lanes = lw // n_pass

    def scan_pass(c):
        ls = slice(c * SCAN_LANES, (c + 1) * SCAN_LANES)
        ll = slice(c * lru_lanes, (c + 1) * lru_lanes)
        ar = jnp.broadcast_to(v("a_re", c * SCAN_LANES, SCAN_LANES), (nb, SCAN_LANES))
        ai = jnp.broadcast_to(v("a_im", c * SCAN_LANES, SCAN_LANES), (nb, SCAN_LANES))

        def scan_step(t, carry):
            sr, si, hl = carry
            rows = pl.ds(pl.multiple_of(t * nb, nb), nb)
            nsr = ar * sr - ai * si + xr[rows, ls]
            nsi = ar * si + ai * sr + xi[rows, ls]
            xr[rows, ls] = nsr
            xi[rows, ls] = nsi
            hn = abuf[rows, ll] * hl + bxbuf[rows, ll]
            bxbuf[rows, ll] = hn
            return nsr, nsi, hn

        sr, si, hl = lax.fori_loop(
            0, tq, scan_step, (s5st[0, :, ls], s5st[1, :, ls], lrust[:, ll]),
            unroll=SCAN_UNROLL)
        s5st[0, :, ls] = sr
        s5st[1, :, ls] = si
        lrust[:, ll] = hl

    def s5_readout(hh):
        ls = slice(hh * half_st, (hh + 1) * half_st)
        return (_dot(xr[:, ls].astype(BF16), ccat_ref[hh, 0])
                + _dot(xi[:, ls].astype(BF16), ccat_ref[hh, 1]))

    for j in range(n_blk):
        lru_block(j)
    ubuf[pl.ds(0, hdr), :] = ubuf[pl.ds(m, hdr), :]
    for c in range(n_pass):
        scan_pass(c)
    y0 = s5_readout(0)
    y1 = s5_readout(1)

    y = jnp.concatenate([y0, y1], axis=1) + v("s5_d") * ua[...]
    z = jax.nn.gelu(y)
    y_a = z * _sigmoid_of_twice(_dot(z.astype(BF16), wglu[...]) + 0.5 * v("b_glu"))
    merged = gates[:, 0:d] * _dot(y_a.astype(BF16), waout[...])
    merged = merged + gates[:, d:2 * d] * _dot(bxbuf[...].astype(BF16), wbout[...])
    delta = _dot(merged.astype(BF16), wo[...])
    o_ref[...] = jnp.swapaxes(delta.reshape(tq, nb, d), 0, 1) + x_ref[...]


def _const_spec(shape):
    nd = len(shape)
    return pl.BlockSpec(shape, lambda i, _nd=nd: (0,) * _nd,
                        pipeline_mode=pl.Buffered(1))


def _mixer_call(x, vec, dense, bc, cc, wc):
    nb, seq, d = x.shape
    m = nb * TQ
    w_in, w_glu, w_a_out, w_b_out, w_o = dense
    s5w = w_glu.shape[1]
    n_half, half_w, _ = bc.shape
    half_st = half_w // S5_GROUP_CH * cc.shape[2]
    nst = n_half * half_st
    lw = w_b_out.shape[1]
    tile_spec = pl.BlockSpec((nb, TQ, d), lambda i: (0, i, 0))
    in_specs = [tile_spec, _const_spec(vec.array.shape)]
    in_specs += [pl.BlockSpec(memory_space=pl.ANY) for _ in dense]
    in_specs += [_const_spec(c.shape) for c in (bc, cc, wc)]
    scratch = [pltpu.VMEM(w.shape[1:], BF16) for w in dense]
    scratch += [
        pltpu.VMEM((STAGE_SLOTS,) + MIXER_STAGE, F32),
        pltpu.SemaphoreType.DMA((STAGE_SLOTS,)),
        pltpu.VMEM((n_half, half_w, 2 * half_st), BF16),
        pltpu.VMEM((n_half, 2, half_st, half_w), BF16),
        pltpu.VMEM((wc.shape[0], MXU_DIM, 2 * MXU_DIM), BF16),
        pltpu.VMEM((m, 2 * d), BF16),
        pltpu.VMEM((m, s5w), F32),
        pltpu.VMEM((m + (CONV_WIDTH - 1) * nb, lw), F32),
        pltpu.VMEM((m, nst), F32),
        pltpu.VMEM((m, nst), F32),
        pltpu.VMEM((2, nb, nst), F32),
        pltpu.VMEM((m, lw), F32),
        pltpu.VMEM((m, lw), F32),
        pltpu.VMEM((nb, lw), F32),
    ]
    return pl.pallas_call(
        functools.partial(_mixer_kernel, vec),
        out_shape=jax.ShapeDtypeStruct(x.shape, F32),
        grid=(seq // TQ,),
        in_specs=in_specs,
        out_specs=tile_spec,
        scratch_shapes=scratch,
        compiler_params=pltpu.CompilerParams(
            dimension_semantics=("arbitrary",), vmem_limit_bytes=VMEM_LIMIT),
        name="mixer",
    )(x, vec.array, *dense, bc, cc, wc)


def _ffn_kernel(vec, x_ref, p_ref, vec_ref, wgate_hbm, wup_hbm, wdown_hbm,
                wpg_hbm, wple_hbm, o_ref,
                wgate, wup, wdown, wpg, wple, stage, sem):
    v = vec.view(vec_ref)

    @pl.when(pl.program_id(0) == 0)
    def _():
        _stream_cast([(wgate_hbm.at[0], wgate, None), (wup_hbm.at[0], wup, None),
                      (wdown_hbm.at[0], wdown, None), (wpg_hbm.at[0], wpg, None),
                      (wple_hbm.at[0], wple, None)], stage, sem)

    hid = wgate.shape[1]
    for s in range(FFN_ROWS // FFN_SUB_ROWS):
        rows = pl.ds(s * FFN_SUB_ROWS, FFN_SUB_ROWS)
        x = x_ref[rows, :]
        h2 = _rms(x, v("g_ffn")).astype(BF16)
        acc = x
        for j in range(hid // FFN_HID_BLK):
            cs = slice(j * FFN_HID_BLK, (j + 1) * FFN_HID_BLK)
            gt = _dot(h2, wgate[:, cs])
            a = gt * _sigmoid(gt) * _dot(h2, wup[:, cs])
            acc = acc + _dot(a.astype(BF16), wdown[cs, :])
        x2 = acc
        gate_p = _sigmoid(
            _dot(_rms(x2, v("g_ple_gate")).astype(BF16), wpg[...]) + v("b_ple_gate"))
        e = _rms(_dot(p_ref[rows, :].astype(BF16), wple[...]), v("g_ple"))
        x3 = x2 + gate_p * e
        o_ref[rows, :] = _rms(x3, v("g_final"))


def _ffn_call(x1, p, vec, dense):
    bsz, seq, d = x1.shape
    pd = p.shape[-1]
    per_seq = seq // FFN_ROWS
    in_specs = [
        pl.BlockSpec((None, FFN_ROWS, d), lambda i: (i // per_seq, i % per_seq, 0)),
        pl.BlockSpec((None, None, FFN_ROWS, pd),
                     lambda i: (0, i // per_seq, i % per_seq, 0)),
        _const_spec(vec.array.shape)]
    in_specs += [pl.BlockSpec(memory_space=pl.ANY) for _ in dense]
    scratch = [pltpu.VMEM(w.shape[1:], BF16) for w in dense]
    scratch += [pltpu.VMEM((STAGE_SLOTS,) + FFN_STAGE, F32),
                pltpu.SemaphoreType.DMA((STAGE_SLOTS,))]
    return pl.pallas_call(
        functools.partial(_ffn_kernel, vec),
        out_shape=jax.ShapeDtypeStruct((bsz, seq, d), F32),
        grid=(bsz * per_seq,),
        in_specs=in_specs,
        out_specs=pl.BlockSpec((None, FFN_ROWS, d),
                               lambda i: (i // per_seq, i % per_seq, 0)),
        scratch_shapes=scratch,
        compiler_params=pltpu.CompilerParams(
            dimension_semantics=("arbitrary",), vmem_limit_bytes=VMEM_LIMIT),
        name="ffn",
    )(x1, p, vec.array, *dense)


def _s5_params(lam_re, lam_im, log_dt, b_re, b_im, c_re, c_im):
    g, n = lam_re.shape
    p = b_re.shape[2]
    hg = g // 2
    dt = jnp.exp(log_dt)[:, None]
    mag = jnp.exp(lam_re * dt)
    ar = mag * jnp.cos(lam_im * dt)
    ai = mag * jnp.sin(lam_im * dt)
    den = lam_re * lam_re + lam_im * lam_im
    nr = ar - 1.0
    fr = (nr * lam_re + ai * lam_im) / den
    fi = (ai * lam_re - nr * lam_im) / den
    bbr = fr[..., None] * b_re - fi[..., None] * b_im
    bbi = fr[..., None] * b_im + fi[..., None] * b_re
    bc = jnp.stack([bbr, bbi]).reshape(2, 2, hg, n, p).transpose(1, 2, 4, 0, 3)
    bc = bc.reshape(2, hg * p, 2 * n)
    cc = jnp.stack([c_re, -c_im]).reshape(2, 2, hg, p, n).transpose(1, 0, 4, 2, 3)
    cc = cc.reshape(2, 2, n, hg * p)
    return ar.reshape(-1), ai.reshape(-1), bc, cc


def _lru_gate_weights(w_r, w_i):
    heads, hd, _ = w_r.shape
    per = MXU_DIM // hd
    tiles = heads // per
    w = jnp.stack([w_r, w_i]).reshape(2, tiles, per, hd, hd).transpose(1, 2, 3, 0, 4)
    return w.reshape(tiles, per * hd, 2 * hd)


def kernel(x, p, g_mix, w_in, b_in, lam_re, lam_im, log_dt, s5_b_re, s5_b_im, s5_c_re, s5_c_im, s5_d, w_glu, b_glu, conv_w, conv_b, w_r, b_r, w_i, b_i, lru_lambda, w_a_out, w_b_out, w_o, g_ffn, w_ffn_gate, w_ffn_up, w_ffn_down, g_ple_gate, w_ple_gate, b_ple_gate, w_ple, g_ple, g_final):
    bsz, seq, d = x.shape
    assert w_in.shape[0] == 1
    assert bsz == SUBLANES and seq % TQ == 0 and seq % FFN_ROWS == 0

    a_re, a_im, bc, cc = _s5_params(
        lam_re[0], lam_im[0], log_dt[0], s5_b_re[0], s5_b_im[0],
        s5_c_re[0], s5_c_im[0])
    mixer_vec = _Packed([
        ("g_mix", g_mix), ("b_in", b_in), ("a_re", a_re), ("a_im", a_im),
        ("s5_d", s5_d), ("b_glu", b_glu), ("conv_w", conv_w), ("conv_b", conv_b),
        ("b_r", b_r), ("b_i", b_i), ("lru_lambda", lru_lambda)])
    x1 = _mixer_call(x, mixer_vec, (w_in, w_glu, w_a_out, w_b_out, w_o),
                     bc, cc, _lru_gate_weights(w_r[0], w_i[0]))
    ffn_vec = _Packed([
        ("g_ffn", g_ffn), ("g_ple_gate", g_ple_gate), ("b_ple_gate", b_ple_gate),
        ("g_ple", g_ple), ("g_final", g_final)])
    return _ffn_call(x1, p, ffn_vec,
                     (w_ffn_gate, w_ffn_up, w_ffn_down, w_ple_gate, w_ple))
```

```python
import functools

import jax
import jax.numpy as jnp
from jax import lax
from jax.experimental import pallas as pl
from jax.experimental.pallas import tpu as pltpu

F32 = jnp.float32
BF16 = jnp.bfloat16

EPS = 1e-6
LRU_C = 8.0
CONV_WIDTH = 4
LRU_HEAD_DIM = 64
S5_GROUP_CH = 16

SUBLANES = 8
LANES = 128
MXU_DIM = 256
TQ = 64
SCAN_LANES = 512
SCAN_UNROLL = 64
STAGE_SLOTS = 6
MIXER_STAGE = (256, 1024)
FFN_ROWS = 1024
FFN_SUB_ROWS = 512
FFN_HID_BLK = 256
FFN_STAGE = (256, 1408)
VMEM_LIMIT = 56 * 1024 * 1024


def _rms(x, g):
    return x * lax.rsqrt(jnp.mean(x * x, axis=-1, keepdims=True) + EPS) * g


def _dot(a, b):
    return jnp.dot(a, b, preferred_element_type=F32)


def _sigmoid_of_twice(half_x):
    return 0.5 * jnp.tanh(half_x) + 0.5


def _sigmoid(x):
    return _sigmoid_of_twice(0.5 * x)


def _sqrt_nonneg(v):
    return jnp.where(v > 0.0, v * lax.rsqrt(v), 0.0)


def _softplus(x):
    return jnp.maximum(x, 0.0) + jnp.log1p(jnp.exp(-jnp.abs(x)))


class _Vectors:
    def __init__(self, named):
        self.names = [name for name, _ in named]
        self.arrays = [a.astype(F32) for _, a in named]
        assert all(a.ndim == 2 and a.shape[1] % LANES == 0 for a in self.arrays)

    def view(self, refs):
        index = {name: i for i, name in enumerate(self.names)}

        def get(name, lo=0, width=None, row=0):
            ref = refs[index[name]]
            width = ref.shape[1] - lo if width is None else width
            return ref[row:row + 1, lo:lo + width]
        return get


def _stream_cast(jobs, stage, sem):
    n_slots, stage_rows, stage_cols = stage.shape
    chunks = []
    for src, dst, scale in jobs:
        rows, cols = dst.shape
        rc = min(rows, stage_rows)
        assert rows % rc == 0 and cols % LANES == 0
        for r0 in range(0, rows, rc):
            for c0 in range(0, cols, stage_cols):
                chunks.append((src, dst, r0, rc, c0, min(stage_cols, cols - c0), scale))

    def copy(i):
        src, _, r0, rc, c0, cw, _ = chunks[i]
        slot = i % n_slots
        return pltpu.make_async_copy(
            src.at[pl.ds(r0, rc), pl.ds(c0, cw)],
            stage.at[slot, pl.ds(0, rc), pl.ds(0, cw)],
            sem.at[slot])

    ahead = n_slots - 1
    for i in range(min(ahead, len(chunks))):
        copy(i).start()
    for i in range(len(chunks)):
        if i + ahead < len(chunks):
            copy(i + ahead).start()
        copy(i).wait()
        _, dst, r0, rc, c0, cw, scale = chunks[i]
        w = stage[i % n_slots, pl.ds(0, rc), pl.ds(0, cw)]
        if scale is not None:
            w = w * scale[:, c0:c0 + cw]
        dst[pl.ds(r0, rc), pl.ds(c0, cw)] = w.astype(BF16)


def _pow2_div(x, n):
    assert n & (n - 1) == 0
    return x >> (n.bit_length() - 1)


def _pow2_mod(x, n):
    assert n & (n - 1) == 0
    return x & (n - 1)


def _block_diag_cols(compact, row_blk, col_blk, n_blocks):
    rows, ccols = compact.shape
    n_parts = ccols // col_blk
    part_w = n_blocks * col_blk
    ocols = n_parts * part_w
    k_i = lax.broadcasted_iota(jnp.int32, (ccols, ocols), 0)
    c_i = lax.broadcasted_iota(jnp.int32, (ccols, ocols), 1)
    onehot = ((_pow2_div(k_i, col_blk) == _pow2_div(c_i, part_w))
              & (_pow2_mod(k_i, col_blk) == _pow2_mod(c_i, col_blk)))
    tiled = _dot(compact.astype(BF16), onehot.astype(BF16))
    r_o = lax.broadcasted_iota(jnp.int32, (rows, ocols), 0)
    c_o = lax.broadcasted_iota(jnp.int32, (rows, ocols), 1)
    keep = _pow2_div(r_o, row_blk) == _pow2_div(_pow2_mod(c_o, part_w), col_blk)
    return jnp.where(keep, tiled, 0.0).astype(BF16)


def _block_diag_rows(compact, row_blk, col_blk, n_blocks):
    tiled = jnp.concatenate([compact] * n_blocks, axis=0)
    r_o = lax.broadcasted_iota(jnp.int32, tiled.shape, 0)
    c_o = lax.broadcasted_iota(jnp.int32, tiled.shape, 1)
    keep = _pow2_div(r_o, row_blk) == _pow2_div(c_o, col_blk)
    return jnp.where(keep, tiled, 0.0).astype(BF16)


def _mixer_kernel(vec, x_ref, *refs):
    n_vec = len(vec.names)
    (win_hbm, wglu_hbm, waout_hbm, wbout_hbm, wo_hbm, bc_ref, cc_ref, wc_ref, o_ref,
     win, wglu, waout, wbout, wo, stage, sem, bbd_ref, ccat_ref, wg_ref,
     gates, ua, ubuf, xr, xi, s5st, abuf, bxbuf, lrust) = refs[n_vec:]
    nb, tq, d = x_ref.shape
    m = nb * tq
    s5w = ua.shape[1]
    nst = xr.shape[1]
    half_w = s5w // 2
    half_st = nst // 2
    lw = abuf.shape[1]
    hdr = (CONV_WIDTH - 1) * nb
    v = vec.view(refs[:n_vec])

    def time_major_norm(tile_ref):
        xt = jnp.swapaxes(tile_ref[...], 0, 1).reshape(m, d)
        return _rms(xt, v("g_mix")).astype(BF16)

    @pl.when(pl.program_id(0) == 0)
    def _():
        col = lax.broadcasted_iota(jnp.int32, (1, win.shape[1]), 1)
        in_scale = jnp.where(col >= s5w + lw, 0.5, 1.0)
        glu_scale = jnp.full((1, wglu.shape[1]), 0.5, F32)
        _stream_cast([(win_hbm.at[0], win, in_scale),
                      (wglu_hbm.at[0], wglu, glu_scale),
                      (waout_hbm.at[0], waout, None), (wbout_hbm.at[0], wbout, None),
                      (wo_hbm.at[0], wo, None)], stage, sem)
        s5_p = S5_GROUP_CH
        s5_n = cc_ref.shape[2]
        n_grp = bc_ref.shape[1] // s5_p
        for hh in range(bbd_ref.shape[0]):
            bbd_ref[hh] = _block_diag_cols(bc_ref[hh], s5_p, s5_n, n_grp)
            for ri in range(2):
                ccat_ref[hh, ri] = _block_diag_rows(cc_ref[hh, ri], s5_n, s5_p, n_grp)
        for j in range(wg_ref.shape[0]):
            wg_ref[j] = _block_diag_cols(0.5 * wc_ref[j], LRU_HEAD_DIM, LRU_HEAD_DIM,
                                         MXU_DIM // LRU_HEAD_DIM)
        s5st[...] = jnp.zeros_like(s5st)
        lrust[...] = jnp.zeros_like(lrust)
        ubuf[pl.ds(0, hdr), :] = jnp.zeros((hdr, lw), F32)

    h = time_major_norm(x_ref)
    neg_sp = -LRU_C * _softplus(-v("lru_lambda"))
    g0 = s5w + lw
    n_blk = lw // MXU_DIM
    gate_w = 2 * d // n_blk

    def in_proj(lo, width, bias_scale=1.0):
        return _dot(h, win[:, lo:lo + width]) + bias_scale * v("b_in", lo, width)

    u = in_proj(0, s5w)
    ua[...] = u
    ub16 = u.astype(BF16)
    def lru_block(j):
        cs = slice(j * MXU_DIM, (j + 1) * MXU_DIM)
        ub = in_proj(s5w + j * MXU_DIM, MXU_DIM)
        ubuf[pl.ds(hdr, m), cs] = ub
        xc = (v("conv_b", j * MXU_DIM, MXU_DIM)
              + v("conv_w", j * MXU_DIM, MXU_DIM, row=CONV_WIDTH - 1) * ub)
        for k in range(CONV_WIDTH - 1):
            xc = xc + (v("conv_w", j * MXU_DIM, MXU_DIM, row=k)
                       * ubuf[pl.ds(k * nb, m), cs])
        if j < 2:
            pr = _dot(ub16[:, j * half_w:(j + 1) * half_w], bbd_ref[j])
            xr[:, j * half_st:(j + 1) * half_st] = pr[:, :half_st]
            xi[:, j * half_st:(j + 1) * half_st] = pr[:, half_st:]
        g = _dot(xc.astype(BF16), wg_ref[j])
        r = _sigmoid_of_twice(g[:, :MXU_DIM] + 0.5 * v("b_r", j * MXU_DIM, MXU_DIM))
        ig = _sigmoid_of_twice(g[:, MXU_DIM:] + 0.5 * v("b_i", j * MXU_DIM, MXU_DIM))
        a = jnp.exp(r * neg_sp[:, cs])
        abuf[:, cs] = a
        bxbuf[:, cs] = _sqrt_nonneg(1.0 - a * a) * ig * xc
        gates[:, j * gate_w:(j + 1) * gate_w] = _sigmoid_of_twice(
            in_proj(g0 + j * gate_w, gate_w, bias_scale=0.5)).astype(BF16)
    n_pass = nst // SCAN_LANES
    lru_lanes = lw // n_pass

    def scan_pass(c):
        ls = slice(c * SCAN_LANES, (c + 1) * SCAN_LANES)
        ll = slice(c * lru_lanes, (c + 1) * lru_lanes)
        ar = jnp.broadcast_to(v("a_re", c * SCAN_LANES, SCAN_LANES), (nb, SCAN_LANES))
        ai = jnp.broadcast_to(v("a_im", c * SCAN_LANES, SCAN_LANES), (nb, SCAN_LANES))

        def scan_step(t, carry):
            sr, si, hl = carry
            rows = pl.ds(pl.multiple_of(t * nb, nb), nb)
            nsr = ar * sr - ai * si + xr[rows, ls]
            nsi = ar * si + ai * sr + xi[rows, ls]
            xr[rows, ls] = nsr
            xi[rows, ls] = nsi
            hn = abuf[rows, ll] * hl + bxbuf[rows, ll]
            bxbuf[rows, ll] = hn
            return nsr, nsi, hn

        sr, si, hl = lax.fori_loop(
            0, tq, scan_step, (s5st[0, :, ls], s5st[1, :, ls], lrust[:, ll]),
            unroll=SCAN_UNROLL)
        s5st[0, :, ls] = sr
        s5st[1, :, ls] = si
        lrust[:, ll] = hl

    def s5_readout(hh):
        ls = slice(hh * half_st, (hh + 1) * half_st)
        return (_dot(xr[:, ls].astype(BF16), ccat_ref[hh, 0])
                + _dot(xi[:, ls].astype(BF16), ccat_ref[hh, 1]))

    for j in range(n_blk):
        lru_block(j)
    ubuf[pl.ds(0, hdr), :] = ubuf[pl.ds(m, hdr), :]
    for c in range(n_pass):
        scan_pass(c)
    y0 = s5_readout(0)
    y1 = s5_readout(1)

    y = jnp.concatenate([y0, y1], axis=1) + v("s5_d") * ua[...]
    z = jax.nn.gelu(y)
    y_a = z * _sigmoid_of_twice(_dot(z.astype(BF16), wglu[...]) + 0.5 * v("b_glu"))
    merged = gates[:, 0:d] * _dot(y_a.astype(BF16), waout[...])
    merged = merged + gates[:, d:2 * d] * _dot(bxbuf[...].astype(BF16), wbout[...])
    delta = _dot(merged.astype(BF16), wo[...])
    o_ref[...] = jnp.swapaxes(delta.reshape(tq, nb, d), 0, 1) + x_ref[...]


def _const_spec(shape):
    nd = len(shape)
    return pl.BlockSpec(shape, lambda i, _nd=nd: (0,) * _nd,
                        pipeline_mode=pl.Buffered(1))


def _mixer_call(x, vec, dense, bc, cc, wc):
    nb, seq, d = x.shape
    m = nb * TQ
    w_in, w_glu, w_a_out, w_b_out, w_o = dense
    s5w = w_glu.shape[1]
    n_half, half_w, _ = bc.shape
    half_st = half_w // S5_GROUP_CH * cc.shape[2]
    nst = n_half * half_st
    lw = w_b_out.shape[1]
    tile_spec = pl.BlockSpec((nb, TQ, d), lambda i: (0, i, 0))
    in_specs = [tile_spec] + [_const_spec(a.shape) for a in vec.arrays]
    in_specs += [pl.BlockSpec(memory_space=pl.ANY) for _ in dense]
    in_specs += [_const_spec(c.shape) for c in (bc, cc, wc)]
    scratch = [pltpu.VMEM(w.shape[1:], BF16) for w in dense]
    scratch += [
        pltpu.VMEM((STAGE_SLOTS,) + MIXER_STAGE, F32),
        pltpu.SemaphoreType.DMA((STAGE_SLOTS,)),
        pltpu.VMEM((n_half, half_w, 2 * half_st), BF16),
        pltpu.VMEM((n_half, 2, half_st, half_w), BF16),
        pltpu.VMEM((wc.shape[0], MXU_DIM, 2 * MXU_DIM), BF16),
        pltpu.VMEM((m, 2 * d), BF16),
        pltpu.VMEM((m, s5w), F32),
        pltpu.VMEM((m + (CONV_WIDTH - 1) * nb, lw), F32),
        pltpu.VMEM((m, nst), F32),
        pltpu.VMEM((m, nst), F32),
        pltpu.VMEM((2, nb, nst), F32),
        pltpu.VMEM((m, lw), F32),
        pltpu.VMEM((m, lw), F32),
        pltpu.VMEM((nb, lw), F32),
    ]
    return pl.pallas_call(
        functools.partial(_mixer_kernel, vec),
        out_shape=jax.ShapeDtypeStruct(x.shape, F32),
        grid=(seq // TQ,),
        in_specs=in_specs,
        out_specs=tile_spec,
        scratch_shapes=scratch,
        compiler_params=pltpu.CompilerParams(
            dimension_semantics=("arbitrary",), vmem_limit_bytes=VMEM_LIMIT),
        name="mixer",
    )(x, *vec.arrays, *dense, bc, cc, wc)


def _ffn_kernel(vec, x_ref, p_ref, *refs):
    n_vec = len(vec.names)
    (wgate_hbm, wup_hbm, wdown_hbm, wpg_hbm, wple_hbm, o_ref,
     wgate, wup, wdown, wpg, wple, stage, sem) = refs[n_vec:]
    v = vec.view(refs[:n_vec])

    @pl.when(pl.program_id(0) == 0)
    def _():
        _stream_cast([(wgate_hbm.at[0], wgate, None), (wup_hbm.at[0], wup, None),
                      (wdown_hbm.at[0], wdown, None), (wpg_hbm.at[0], wpg, None),
                      (wple_hbm.at[0], wple, None)], stage, sem)

    hid = wgate.shape[1]
    for s in range(FFN_ROWS // FFN_SUB_ROWS):
        rows = pl.ds(s * FFN_SUB_ROWS, FFN_SUB_ROWS)
        x = x_ref[rows, :]
        h2 = _rms(x, v("g_ffn")).astype(BF16)
        acc = x
        for j in range(hid // FFN_HID_BLK):
            cs = slice(j * FFN_HID_BLK, (j + 1) * FFN_HID_BLK)
            gt = _dot(h2, wgate[:, cs])
            a = gt * _sigmoid(gt) * _dot(h2, wup[:, cs])
            acc = acc + _dot(a.astype(BF16), wdown[cs, :])
        x2 = acc
        gate_p = _sigmoid(
            _dot(_rms(x2, v("g_ple_gate")).astype(BF16), wpg[...]) + v("b_ple_gate"))
        e = _rms(_dot(p_ref[rows, :].astype(BF16), wple[...]), v("g_ple"))
        x3 = x2 + gate_p * e
        o_ref[rows, :] = _rms(x3, v("g_final"))


def _ffn_call(x1, p, vec, dense):
    bsz, seq, d = x1.shape
    pd = p.shape[-1]
    per_seq = seq // FFN_ROWS
    in_specs = [
        pl.BlockSpec((None, FFN_ROWS, d), lambda i: (i // per_seq, i % per_seq, 0)),
        pl.BlockSpec((None, None, FFN_ROWS, pd),
                     lambda i: (0, i // per_seq, i % per_seq, 0))]
    in_specs += [_const_spec(a.shape) for a in vec.arrays]
    in_specs += [pl.BlockSpec(memory_space=pl.ANY) for _ in dense]
    scratch = [pltpu.VMEM(w.shape[1:], BF16) for w in dense]
    scratch += [pltpu.VMEM((STAGE_SLOTS,) + FFN_STAGE, F32),
                pltpu.SemaphoreType.DMA((STAGE_SLOTS,))]
    return pl.pallas_call(
        functools.partial(_ffn_kernel, vec),
        out_shape=jax.ShapeDtypeStruct((bsz, seq, d), F32),
        grid=(bsz * per_seq,),
        in_specs=in_specs,
        out_specs=pl.BlockSpec((None, FFN_ROWS, d),
                               lambda i: (i // per_seq, i % per_seq, 0)),
        scratch_shapes=scratch,
        compiler_params=pltpu.CompilerParams(
            dimension_semantics=("arbitrary",), vmem_limit_bytes=VMEM_LIMIT),
        name="ffn",
    )(x1, p, *vec.arrays, *dense)


def _s5_params(lam_re, lam_im, log_dt, b_re, b_im, c_re, c_im):
    g, n = lam_re.shape
    p = b_re.shape[2]
    hg = g // 2
    dt = jnp.exp(log_dt)[:, None]
    mag = jnp.exp(lam_re * dt)
    ar = mag * jnp.cos(lam_im * dt)
    ai = mag * jnp.sin(lam_im * dt)
    den = lam_re * lam_re + lam_im * lam_im
    nr = ar - 1.0
    fr = (nr * lam_re + ai * lam_im) / den
    fi = (ai * lam_re - nr * lam_im) / den
    bbr = fr[..., None] * b_re - fi[..., None] * b_im
    bbi = fr[..., None] * b_im + fi[..., None] * b_re
    bc = jnp.stack([bbr, bbi]).reshape(2, 2, hg, n, p).transpose(1, 2, 4, 0, 3)
    bc = bc.reshape(2, hg * p, 2 * n)
    cc = jnp.stack([c_re, -c_im]).reshape(2, 2, hg, p, n).transpose(1, 0, 4, 2, 3)
    cc = cc.reshape(2, 2, n, hg * p)
    return ar.reshape(-1), ai.reshape(-1), bc, cc


def _lru_gate_weights(w_r, w_i):
    heads, hd, _ = w_r.shape
    per = MXU_DIM // hd
    tiles = heads // per
    w = jnp.stack([w_r, w_i]).reshape(2, tiles, per, hd, hd).transpose(1, 2, 3, 0, 4)
    return w.reshape(tiles, per * hd, 2 * hd)


def kernel(x, p, g_mix, w_in, b_in, lam_re, lam_im, log_dt, s5_b_re, s5_b_im, s5_c_re, s5_c_im, s5_d, w_glu, b_glu, conv_w, conv_b, w_r, b_r, w_i, b_i, lru_lambda, w_a_out, w_b_out, w_o, g_ffn, w_ffn_gate, w_ffn_up, w_ffn_down, g_ple_gate, w_ple_gate, b_ple_gate, w_ple, g_ple, g_final):
    bsz, seq, d = x.shape
    assert w_in.shape[0] == 1
    assert bsz == SUBLANES and seq % TQ == 0 and seq % FFN_ROWS == 0

    a_re, a_im, bc, cc = _s5_params(
        lam_re[0], lam_im[0], log_dt[0], s5_b_re[0], s5_b_im[0],
        s5_c_re[0], s5_c_im[0])
    def row(a):
        return a.reshape(1, -1)

    mixer_vec = _Vectors([
        ("g_mix", g_mix), ("b_in", b_in), ("a_re", row(a_re)), ("a_im", row(a_im)),
        ("s5_d", row(s5_d)), ("b_glu", b_glu), ("conv_w", conv_w[0]),
        ("conv_b", conv_b), ("b_r", row(b_r)), ("b_i", row(b_i)),
        ("lru_lambda", lru_lambda)])
    x1 = _mixer_call(x, mixer_vec, (w_in, w_glu, w_a_out, w_b_out, w_o),
                     bc, cc, _lru_gate_weights(w_r[0], w_i[0]))
    ffn_vec = _Vectors([
        ("g_ffn", g_ffn), ("g_ple_gate", g_ple_gate), ("b_ple_gate", b_ple_gate),
        ("g_ple", g_ple), ("g_final", row(g_final))])
    return _ffn_call(x1, p, ffn_vec,
                     (w_ffn_gate, w_ffn_up, w_ffn_down, w_ple_gate, w_ple))
```

```python
import functools

import jax
import jax.numpy as jnp
from jax import lax
from jax.experimental import pallas as pl
from jax.experimental.pallas import tpu as pltpu

F32 = jnp.float32
BF16 = jnp.bfloat16

EPS = 1e-6
LRU_C = 8.0
CONV_WIDTH = 4
LRU_HEAD_DIM = 64
S5_GROUP_CH = 16

SUBLANES = 8
LANES = 128
MXU_DIM = 256
TQ = 64
SCAN_LANES = 512
SCAN_UNROLL = 64
STAGE_SLOTS = 6
MIXER_STAGE = (256, 1024)
FFN_ROWS = 1024
FFN_SUB_ROWS = 512
FFN_HID_BLK = 256
FFN_STAGE = (256, 1408)
VMEM_LIMIT = 56 * 1024 * 1024


def _rms(x, g):
    return x * lax.rsqrt(jnp.mean(x * x, axis=-1, keepdims=True) + EPS) * g


def _dot(a, b):
    return jnp.dot(a, b, preferred_element_type=F32)


def _sigmoid_of_twice(half_x):
    return 0.5 * jnp.tanh(half_x) + 0.5


def _sigmoid(x):
    return _sigmoid_of_twice(0.5 * x)


def _sqrt_nonneg(v):
    return jnp.where(v > 0.0, v * lax.rsqrt(v), 0.0)


def _softplus(x):
    return jnp.maximum(x, 0.0) + jnp.log1p(jnp.exp(-jnp.abs(x)))


class _Vectors:
    def __init__(self, named):
        self.names = [name for name, _ in named]
        self.arrays = [a.astype(F32) for _, a in named]
        assert all(a.ndim == 2 and a.shape[1] % LANES == 0 for a in self.arrays)

    def view(self, refs):
        index = {name: i for i, name in enumerate(self.names)}

        def get(name, lo=0, width=None, row=0):
            ref = refs[index[name]]
            width = ref.shape[1] - lo if width is None else width
            return ref[row:row + 1, lo:lo + width]
        return get


def _stream_cast(jobs, stage, sem):
    n_slots, stage_rows, stage_cols = stage.shape
    chunks = []
    for src, dst, scale in jobs:
        rows, cols = dst.shape
        rc = min(rows, stage_rows)
        assert rows % rc == 0 and cols % LANES == 0
        for r0 in range(0, rows, rc):
            for c0 in range(0, cols, stage_cols):
                chunks.append((src, dst, r0, rc, c0, min(stage_cols, cols - c0), scale))

    def copy(i):
        src, _, r0, rc, c0, cw, _ = chunks[i]
        slot = i % n_slots
        return pltpu.make_async_copy(
            src.at[pl.ds(r0, rc), pl.ds(c0, cw)],
            stage.at[slot, pl.ds(0, rc), pl.ds(0, cw)],
            sem.at[slot])

    ahead = n_slots - 1
    for i in range(min(ahead, len(chunks))):
        copy(i).start()
    for i in range(len(chunks)):
        if i + ahead < len(chunks):
            copy(i + ahead).start()
        copy(i).wait()
        _, dst, r0, rc, c0, cw, scale = chunks[i]
        w = stage[i % n_slots, pl.ds(0, rc), pl.ds(0, cw)]
        if scale is not None:
            w = w * scale[:, c0:c0 + cw]
        dst[pl.ds(r0, rc), pl.ds(c0, cw)] = w.astype(BF16)


def _pow2_div(x, n):
    assert n & (n - 1) == 0
    return x >> (n.bit_length() - 1)


def _pow2_mod(x, n):
    assert n & (n - 1) == 0
    return x & (n - 1)


def _block_diag_cols(compact, row_blk, col_blk, n_blocks):
    rows, ccols = compact.shape
    n_parts = ccols // col_blk
    part_w = n_blocks * col_blk
    ocols = n_parts * part_w
    k_i = lax.broadcasted_iota(jnp.int32, (ccols, ocols), 0)
    c_i = lax.broadcasted_iota(jnp.int32, (ccols, ocols), 1)
    onehot = ((_pow2_div(k_i, col_blk) == _pow2_div(c_i, part_w))
              & (_pow2_mod(k_i, col_blk) == _pow2_mod(c_i, col_blk)))
    tiled = _dot(compact.astype(BF16), onehot.astype(BF16))
    r_o = lax.broadcasted_iota(jnp.int32, (rows, ocols), 0)
    c_o = lax.broadcasted_iota(jnp.int32, (rows, ocols), 1)
    keep = _pow2_div(r_o, row_blk) == _pow2_div(_pow2_mod(c_o, part_w), col_blk)
    return jnp.where(keep, tiled, 0.0).astype(BF16)


def _block_diag_rows(compact, row_blk, col_blk, n_blocks):
    tiled = jnp.concatenate([compact] * n_blocks, axis=0)
    r_o = lax.broadcasted_iota(jnp.int32, tiled.shape, 0)
    c_o = lax.broadcasted_iota(jnp.int32, tiled.shape, 1)
    keep = _pow2_div(r_o, row_blk) == _pow2_div(c_o, col_blk)
    return jnp.where(keep, tiled, 0.0).astype(BF16)


def _mixer_kernel(vec, x_ref, *refs):
    n_vec = len(vec.names)
    (win_hbm, wglu_hbm, waout_hbm, wbout_hbm, wo_hbm, bc_ref, cc_ref, wc_ref, o_ref,
     win, wglu, waout, wbout, wo, stage, sem, bbd_ref, ccat_ref, wg_ref,
     gates, ua, ubuf, xr, xi, s5st, abuf, bxbuf, lrust) = refs[n_vec:]
    nb, tq, d = x_ref.shape
    m = nb * tq
    s5w = ua.shape[1]
    nst = xr.shape[1]
    half_w = s5w // 2
    half_st = nst // 2
    lw = abuf.shape[1]
    hdr = (CONV_WIDTH - 1) * nb
    v = vec.view(refs[:n_vec])

    def time_major_norm(tile_ref):
        xt = jnp.swapaxes(tile_ref[...], 0, 1).reshape(m, d)
        return _rms(xt, v("g_mix")).astype(BF16)

    @pl.when(pl.program_id(0) == 0)
    def _():
        col = lax.broadcasted_iota(jnp.int32, (1, win.shape[1]), 1)
        in_scale = jnp.where(col >= s5w + lw, 0.5, 1.0)
        glu_scale = jnp.full((1, wglu.shape[1]), 0.5, F32)
        _stream_cast([(win_hbm.at[0], win, in_scale),
                      (wglu_hbm.at[0], wglu, glu_scale),
                      (waout_hbm.at[0], waout, None), (wbout_hbm.at[0], wbout, None),
                      (wo_hbm.at[0], wo, None)], stage, sem)
        s5_p = S5_GROUP_CH
        s5_n = cc_ref.shape[2]
        n_grp = bc_ref.shape[1] // s5_p
        for hh in range(bbd_ref.shape[0]):
            bbd_ref[hh] = _block_diag_cols(bc_ref[hh], s5_p, s5_n, n_grp)
            for ri in range(2):
                ccat_ref[hh, ri] = _block_diag_rows(cc_ref[hh, ri], s5_n, s5_p, n_grp)
        for j in range(wg_ref.shape[0]):
            wg_ref[j] = _block_diag_cols(0.5 * wc_ref[j], LRU_HEAD_DIM, LRU_HEAD_DIM,
                                         MXU_DIM // LRU_HEAD_DIM)
        s5st[...] = jnp.zeros_like(s5st)
        lrust[...] = jnp.zeros_like(lrust)
        ubuf[pl.ds(0, hdr), :] = jnp.zeros((hdr, lw), F32)

    h = time_major_norm(x_ref)
    neg_sp = -LRU_C * _softplus(-v("lru_lambda"))
    g0 = s5w + lw
    n_blk = lw // MXU_DIM
    gate_w = 2 * d // n_blk

    def in_proj(lo, width, bias_scale=1.0):
        return _dot(h, win[:, lo:lo + width]) + bias_scale * v("b_in", lo, width)

    u = in_proj(0, s5w)
    ua[...] = u
    ub16 = u.astype(BF16)
    ub_all = in_proj(s5w, lw)
    ubuf[pl.ds(hdr, m), :] = ub_all

    def lru_block(j):
        cs = slice(j * MXU_DIM, (j + 1) * MXU_DIM)
        ub = ub_all[:, cs]
        xc = (v("conv_b", j * MXU_DIM, MXU_DIM)
              + v("conv_w", j * MXU_DIM, MXU_DIM, row=CONV_WIDTH - 1) * ub)
        for k in range(CONV_WIDTH - 1):
            xc = xc + (v("conv_w", j * MXU_DIM, MXU_DIM, row=k)
                       * ubuf[pl.ds(k * nb, m), cs])
        if j < 2:
            pr = _dot(ub16[:, j * half_w:(j + 1) * half_w], bbd_ref[j])
            xr[:, j * half_st:(j + 1) * half_st] = pr[:, :half_st]
            xi[:, j * half_st:(j + 1) * half_st] = pr[:, half_st:]
        g = _dot(xc.astype(BF16), wg_ref[j])
        r = _sigmoid_of_twice(g[:, :MXU_DIM] + 0.5 * v("b_r", j * MXU_DIM, MXU_DIM))
        ig = _sigmoid_of_twice(g[:, MXU_DIM:] + 0.5 * v("b_i", j * MXU_DIM, MXU_DIM))
        a = jnp.exp(r * neg_sp[:, cs])
        abuf[:, cs] = a
        bxbuf[:, cs] = _sqrt_nonneg(1.0 - a * a) * ig * xc
        gates[:, j * gate_w:(j + 1) * gate_w] = _sigmoid_of_twice(
            in_proj(g0 + j * gate_w, gate_w, bias_scale=0.5)).astype(BF16)
    n_pass = nst // SCAN_LANES
    lru_lanes = lw // n_pass

    def scan_pass(c):
        ls = slice(c * SCAN_LANES, (c + 1) * SCAN_LANES)
        ll = slice(c * lru_lanes, (c + 1) * lru_lanes)
        ar = jnp.broadcast_to(v("a_re", c * SCAN_LANES, SCAN_LANES), (nb, SCAN_LANES))
        ai = jnp.broadcast_to(v("a_im", c * SCAN_LANES, SCAN_LANES), (nb, SCAN_LANES))

        def scan_step(t, carry):
            sr, si, hl = carry
            rows = pl.ds(pl.multiple_of(t * nb, nb), nb)
            nsr = ar * sr - ai * si + xr[rows, ls]
            nsi = ar * si + ai * sr + xi[rows, ls]
            xr[rows, ls] = nsr
            xi[rows, ls] = nsi
            hn = abuf[rows, ll] * hl + bxbuf[rows, ll]
            bxbuf[rows, ll] = hn
            return nsr, nsi, hn

        sr, si, hl = lax.fori_loop(
            0, tq, scan_step, (s5st[0, :, ls], s5st[1, :, ls], lrust[:, ll]),
            unroll=SCAN_UNROLL)
        s5st[0, :, ls] = sr
        s5st[1, :, ls] = si
        lrust[:, ll] = hl

    def s5_readout(hh):
        ls = slice(hh * half_st, (hh + 1) * half_st)
        return (_dot(xr[:, ls].astype(BF16), ccat_ref[hh, 0])
                + _dot(xi[:, ls].astype(BF16), ccat_ref[hh, 1]))

    for j in range(n_blk):
        lru_block(j)
    ubuf[pl.ds(0, hdr), :] = ubuf[pl.ds(m, hdr), :]
    for c in range(n_pass):
        scan_pass(c)
    y0 = s5_readout(0)
    y1 = s5_readout(1)

    y = jnp.concatenate([y0, y1], axis=1) + v("s5_d") * ua[...]
    z = jax.nn.gelu(y)
    y_a = z * _sigmoid_of_twice(_dot(z.astype(BF16), wglu[...]) + 0.5 * v("b_glu"))
    merged = gates[:, 0:d] * _dot(y_a.astype(BF16), waout[...])
    merged = merged + gates[:, d:2 * d] * _dot(bxbuf[...].astype(BF16), wbout[...])
    delta = _dot(merged.astype(BF16), wo[...])
    o_ref[...] = jnp.swapaxes(delta.reshape(tq, nb, d), 0, 1) + x_ref[...]


def _const_spec(shape):
    nd = len(shape)
    return pl.BlockSpec(shape, lambda i, _nd=nd: (0,) * _nd,
                        pipeline_mode=pl.Buffered(1))


def _mixer_call(x, vec, dense, bc, cc, wc):
    nb, seq, d = x.shape
    m = nb * TQ
    w_in, w_glu, w_a_out, w_b_out, w_o = dense
    s5w = w_glu.shape[1]
    n_half, half_w, _ = bc.shape
    half_st = half_w // S5_GROUP_CH * cc.shape[2]
    nst = n_half * half_st
    lw = w_b_out.shape[1]
    tile_spec = pl.BlockSpec((nb, TQ, d), lambda i: (0, i, 0))
    in_specs = [tile_spec] + [_const_spec(a.shape) for a in vec.arrays]
    in_specs += [pl.BlockSpec(memory_space=pl.ANY) for _ in dense]
    in_specs += [_const_spec(c.shape) for c in (bc, cc, wc)]
    scratch = [pltpu.VMEM(w.shape[1:], BF16) for w in dense]
    scratch += [
        pltpu.VMEM((STAGE_SLOTS,) + MIXER_STAGE, F32),
        pltpu.SemaphoreType.DMA((STAGE_SLOTS,)),
        pltpu.VMEM((n_half, half_w, 2 * half_st), BF16),
        pltpu.VMEM((n_half, 2, half_st, half_w), BF16),
        pltpu.VMEM((wc.shape[0], MXU_DIM, 2 * MXU_DIM), BF16),
        pltpu.VMEM((m, 2 * d), BF16),
        pltpu.VMEM((m, s5w), F32),
        pltpu.VMEM((m + (CONV_WIDTH - 1) * nb, lw), F32),
        pltpu.VMEM((m, nst), F32),
        pltpu.VMEM((m, nst), F32),
        pltpu.VMEM((2, nb, nst), F32),
        pltpu.VMEM((m, lw), F32),
        pltpu.VMEM((m, lw), F32),
        pltpu.VMEM((nb, lw), F32),
    ]
    return pl.pallas_call(
        functools.partial(_mixer_kernel, vec),
        out_shape=jax.ShapeDtypeStruct(x.shape, F32),
        grid=(seq // TQ,),
        in_specs=in_specs,
        out_specs=tile_spec,
        scratch_shapes=scratch,
        compiler_params=pltpu.CompilerParams(
            dimension_semantics=("arbitrary",), vmem_limit_bytes=VMEM_LIMIT),
        name="mixer",
    )(x, *vec.arrays, *dense, bc, cc, wc)


def _ffn_kernel(vec, x_ref, p_ref, *refs):
    n_vec = len(vec.names)
    (wgate_hbm, wup_hbm, wdown_hbm, wpg_hbm, wple_hbm, o_ref,
     wgate, wup, wdown, wpg, wple, stage, sem) = refs[n_vec:]
    v = vec.view(refs[:n_vec])

    @pl.when(pl.program_id(0) == 0)
    def _():
        _stream_cast([(wgate_hbm.at[0], wgate, None), (wup_hbm.at[0], wup, None),
                      (wdown_hbm.at[0], wdown, None), (wpg_hbm.at[0], wpg, None),
                      (wple_hbm.at[0], wple, None)], stage, sem)

    hid = wgate.shape[1]
    for s in range(FFN_ROWS // FFN_SUB_ROWS):
        rows = pl.ds(s * FFN_SUB_ROWS, FFN_SUB_ROWS)
        x = x_ref[rows, :]
        h2 = _rms(x, v("g_ffn")).astype(BF16)
        acc = x
        for j in range(hid // FFN_HID_BLK):
            cs = slice(j * FFN_HID_BLK, (j + 1) * FFN_HID_BLK)
            gt = _dot(h2, wgate[:, cs])
            a = gt * _sigmoid(gt) * _dot(h2, wup[:, cs])
            acc = acc + _dot(a.astype(BF16), wdown[cs, :])
        x2 = acc
        gate_p = _sigmoid(
            _dot(_rms(x2, v("g_ple_gate")).astype(BF16), wpg[...]) + v("b_ple_gate"))
        e = _rms(_dot(p_ref[rows, :].astype(BF16), wple[...]), v("g_ple"))
        x3 = x2 + gate_p * e
        o_ref[rows, :] = _rms(x3, v("g_final"))


def _ffn_call(x1, p, vec, dense):
    bsz, seq, d = x1.shape
    pd = p.shape[-1]
    per_seq = seq // FFN_ROWS
    in_specs = [
        pl.BlockSpec((None, FFN_ROWS, d), lambda i: (i // per_seq, i % per_seq, 0)),
        pl.BlockSpec((None, None, FFN_ROWS, pd),
                     lambda i: (0, i // per_seq, i % per_seq, 0))]
    in_specs += [_const_spec(a.shape) for a in vec.arrays]
    in_specs += [pl.BlockSpec(memory_space=pl.ANY) for _ in dense]
    scratch = [pltpu.VMEM(w.shape[1:], BF16) for w in dense]
    scratch += [pltpu.VMEM((STAGE_SLOTS,) + FFN_STAGE, F32),
                pltpu.SemaphoreType.DMA((STAGE_SLOTS,))]
    return pl.pallas_call(
        functools.partial(_ffn_kernel, vec),
        out_shape=jax.ShapeDtypeStruct((bsz, seq, d), F32),
        grid=(bsz * per_seq,),
        in_specs=in_specs,
        out_specs=pl.BlockSpec((None, FFN_ROWS, d),
                               lambda i: (i // per_seq, i % per_seq, 0)),
        scratch_shapes=scratch,
        compiler_params=pltpu.CompilerParams(
            dimension_semantics=("arbitrary",), vmem_limit_bytes=VMEM_LIMIT),
        name="ffn",
    )(x1, p, *vec.arrays, *dense)


def _s5_params(lam_re, lam_im, log_dt, b_re, b_im, c_re, c_im):
    g, n = lam_re.shape
    p = b_re.shape[2]
    hg = g // 2
    dt = jnp.exp(log_dt)[:, None]
    mag = jnp.exp(lam_re * dt)
    ar = mag * jnp.cos(lam_im * dt)
    ai = mag * jnp.sin(lam_im * dt)
    den = lam_re * lam_re + lam_im * lam_im
    nr = ar - 1.0
    fr = (nr * lam_re + ai * lam_im) / den
    fi = (ai * lam_re - nr * lam_im) / den
    bbr = fr[..., None] * b_re - fi[..., None] * b_im
    bbi = fr[..., None] * b_im + fi[..., None] * b_re
    bc = jnp.stack([bbr, bbi]).reshape(2, 2, hg, n, p).transpose(1, 2, 4, 0, 3)
    bc = bc.reshape(2, hg * p, 2 * n)
    cc = jnp.stack([c_re, -c_im]).reshape(2, 2, hg, p, n).transpose(1, 0, 4, 2, 3)
    cc = cc.reshape(2, 2, n, hg * p)
    return ar.reshape(-1), ai.reshape(-1), bc, cc


def _lru_gate_weights(w_r, w_i):
    heads, hd, _ = w_r.shape
    per = MXU_DIM // hd
    tiles = heads // per
    w = jnp.stack([w_r, w_i]).reshape(2, tiles, per, hd, hd).transpose(1, 2, 3, 0, 4)
    return w.reshape(tiles, per * hd, 2 * hd)


def kernel(x, p, g_mix, w_in, b_in, lam_re, lam_im, log_dt, s5_b_re, s5_b_im, s5_c_re, s5_c_im, s5_d, w_glu, b_glu, conv_w, conv_b, w_r, b_r, w_i, b_i, lru_lambda, w_a_out, w_b_out, w_o, g_ffn, w_ffn_gate, w_ffn_up, w_ffn_down, g_ple_gate, w_ple_gate, b_ple_gate, w_ple, g_ple, g_final):
    bsz, seq, d = x.shape
    assert w_in.shape[0] == 1
    assert bsz == SUBLANES and seq % TQ == 0 and seq % FFN_ROWS == 0

    a_re, a_im, bc, cc = _s5_params(
        lam_re[0], lam_im[0], log_dt[0], s5_b_re[0], s5_b_im[0],
        s5_c_re[0], s5_c_im[0])
    def row(a):
        return a.reshape(1, -1)

    mixer_vec = _Vectors([
        ("g_mix", g_mix), ("b_in", b_in), ("a_re", row(a_re)), ("a_im", row(a_im)),
        ("s5_d", row(s5_d)), ("b_glu", b_glu), ("conv_w", conv_w[0]),
        ("conv_b", conv_b), ("b_r", row(b_r)), ("b_i", row(b_i)),
        ("lru_lambda", lru_lambda)])
    x1 = _mixer_call(x, mixer_vec, (w_in, w_glu, w_a_out, w_b_out, w_o),
                     bc, cc, _lru_gate_weights(w_r[0], w_i[0]))
    ffn_vec = _Vectors([
        ("g_ffn", g_ffn), ("g_ple_gate", g_ple_gate), ("b_ple_gate", b_ple_gate),
        ("g_ple", g_ple), ("g_final", row(g_final))])
    return _ffn_call(x1, p, ffn_vec,
                     (w_ffn_gate, w_ffn_up, w_ffn_down, w_ple_gate, w_ple))
```

```python
import functools

import jax
import jax.numpy as jnp
from jax import lax
from jax.experimental import pallas as pl
from jax.experimental.pallas import tpu as pltpu

F32 = jnp.float32
BF16 = jnp.bfloat16

EPS = 1e-6
LRU_C = 8.0
CONV_WIDTH = 4
LRU_HEAD_DIM = 64
S5_GROUP_CH = 16

SUBLANES = 8
LANES = 128
MXU_DIM = 256
TQ = 64
SCAN_LANES = 512
SCAN_UNROLL = 64
STAGE_SLOTS = 6
MIXER_STAGE = (256, 1024)
FFN_ROWS = 1024
FFN_SUB_ROWS = 512
FFN_HID_BLK = 256
FFN_STAGE = (256, 1408)
VMEM_LIMIT = 56 * 1024 * 1024


def _rms(x, g):
    return x * lax.rsqrt(jnp.mean(x * x, axis=-1, keepdims=True) + EPS) * g


def _dot(a, b):
    return jnp.dot(a, b, preferred_element_type=F32)


def _sigmoid_of_twice(half_x):
    return 0.5 * jnp.tanh(half_x) + 0.5


def _sigmoid(x):
    return _sigmoid_of_twice(0.5 * x)


def _sqrt_nonneg(v):
    return jnp.where(v > 0.0, v * lax.rsqrt(v), 0.0)


def _softplus(x):
    return jnp.maximum(x, 0.0) + jnp.log1p(jnp.exp(-jnp.abs(x)))


class _Vectors:
    def __init__(self, named):
        self.names = [name for name, _ in named]
        self.arrays = [a.astype(F32) for _, a in named]
        assert all(a.ndim == 2 and a.shape[1] % LANES == 0 for a in self.arrays)

    def view(self, refs):
        index = {name: i for i, name in enumerate(self.names)}

        def get(name, lo=0, width=None, row=0):
            ref = refs[index[name]]
            width = ref.shape[1] - lo if width is None else width
            return ref[row:row + 1, lo:lo + width]
        return get


def _stream_cast(jobs, stage, sem):
    n_slots, stage_rows, stage_cols = stage.shape
    chunks = []
    for src, dst, scale in jobs:
        rows, cols = dst.shape
        rc = min(rows, stage_rows)
        assert rows % rc == 0 and cols % LANES == 0
        for r0 in range(0, rows, rc):
            for c0 in range(0, cols, stage_cols):
                chunks.append((src, dst, r0, rc, c0, min(stage_cols, cols - c0), scale))

    def copy(i):
        src, _, r0, rc, c0, cw, _ = chunks[i]
        slot = i % n_slots
        return pltpu.make_async_copy(
            src.at[pl.ds(r0, rc), pl.ds(c0, cw)],
            stage.at[slot, pl.ds(0, rc), pl.ds(0, cw)],
            sem.at[slot])

    ahead = n_slots - 1
    for i in range(min(ahead, len(chunks))):
        copy(i).start()
    for i in range(len(chunks)):
        if i + ahead < len(chunks):
            copy(i + ahead).start()
        copy(i).wait()
        _, dst, r0, rc, c0, cw, scale = chunks[i]
        w = stage[i % n_slots, pl.ds(0, rc), pl.ds(0, cw)]
        if scale is not None:
            w = w * scale[:, c0:c0 + cw]
        dst[pl.ds(r0, rc), pl.ds(c0, cw)] = w.astype(BF16)


def _pow2_div(x, n):
    assert n & (n - 1) == 0
    return x >> (n.bit_length() - 1)


def _pow2_mod(x, n):
    assert n & (n - 1) == 0
    return x & (n - 1)


def _block_diag_cols(compact, row_blk, col_blk, n_blocks):
    rows, ccols = compact.shape
    n_parts = ccols // col_blk
    part_w = n_blocks * col_blk
    ocols = n_parts * part_w
    k_i = lax.broadcasted_iota(jnp.int32, (ccols, ocols), 0)
    c_i = lax.broadcasted_iota(jnp.int32, (ccols, ocols), 1)
    onehot = ((_pow2_div(k_i, col_blk) == _pow2_div(c_i, part_w))
              & (_pow2_mod(k_i, col_blk) == _pow2_mod(c_i, col_blk)))
    tiled = _dot(compact.astype(BF16), onehot.astype(BF16))
    r_o = lax.broadcasted_iota(jnp.int32, (rows, ocols), 0)
    c_o = lax.broadcasted_iota(jnp.int32, (rows, ocols), 1)
    keep = _pow2_div(r_o, row_blk) == _pow2_div(_pow2_mod(c_o, part_w), col_blk)
    return jnp.where(keep, tiled, 0.0).astype(BF16)


def _block_diag_rows(compact, row_blk, col_blk, n_blocks):
    tiled = jnp.concatenate([compact] * n_blocks, axis=0)
    r_o = lax.broadcasted_iota(jnp.int32, tiled.shape, 0)
    c_o = lax.broadcasted_iota(jnp.int32, tiled.shape, 1)
    keep = _pow2_div(r_o, row_blk) == _pow2_div(c_o, col_blk)
    return jnp.where(keep, tiled, 0.0).astype(BF16)


def _mixer_kernel(vec, x_hbm, *refs):
    n_vec = len(vec.names)
    (win_hbm, wglu_hbm, waout_hbm, wbout_hbm, wo_hbm, bc_ref, cc_ref, wc_ref, o_hbm,
     win, wglu, waout, wbout, wo, stage, sem, bbd_ref, ccat_ref, wg_ref,
     xbuf, obuf, in_sem, out_sem,
     gates, ua, ubuf, xr, xi, s5st, abuf, bxbuf, lrust) = refs[n_vec:]
    _, tq, nb, d = xbuf.shape
    m = nb * tq
    s5w = ua.shape[1]
    nst = xr.shape[1]
    half_w = s5w // 2
    half_st = nst // 2
    lw = abuf.shape[1]
    hdr = (CONV_WIDTH - 1) * nb
    v = vec.view(refs[:n_vec])
    step = pl.program_id(0)
    n_steps = pl.num_programs(0)
    slot = step % 2

    def in_copy(tile, slot_, b):
        return pltpu.make_async_copy(
            x_hbm.at[b, pl.ds(tile * tq, tq), :], xbuf.at[slot_, :, b, :],
            in_sem.at[slot_, b])

    def out_copy(tile, slot_, b):
        return pltpu.make_async_copy(
            obuf.at[slot_, :, b, :], o_hbm.at[b, pl.ds(tile * tq, tq), :],
            out_sem.at[slot_, b])

    @pl.when(step == 0)
    def _():
        for b in range(nb):
            in_copy(0, 0, b).start()

    @pl.when(step + 1 < n_steps)
    def _():
        for b in range(nb):
            in_copy(step + 1, 1 - slot, b).start()

    @pl.when(step >= 2)
    def _():
        for b in range(nb):
            out_copy(step - 2, slot, b).wait()

    @pl.when(pl.program_id(0) == 0)
    def _():
        col = lax.broadcasted_iota(jnp.int32, (1, win.shape[1]), 1)
        in_scale = jnp.where(col >= s5w + lw, 0.5, 1.0)
        glu_scale = jnp.full((1, wglu.shape[1]), 0.5, F32)
        _stream_cast([(win_hbm.at[0], win, in_scale),
                      (wglu_hbm.at[0], wglu, glu_scale),
                      (waout_hbm.at[0], waout, None), (wbout_hbm.at[0], wbout, None),
                      (wo_hbm.at[0], wo, None)], stage, sem)
        s5_p = S5_GROUP_CH
        s5_n = cc_ref.shape[2]
        n_grp = bc_ref.shape[1] // s5_p
        for hh in range(bbd_ref.shape[0]):
            bbd_ref[hh] = _block_diag_cols(bc_ref[hh], s5_p, s5_n, n_grp)
            for ri in range(2):
                ccat_ref[hh, ri] = _block_diag_rows(cc_ref[hh, ri], s5_n, s5_p, n_grp)
        for j in range(wg_ref.shape[0]):
            wg_ref[j] = _block_diag_cols(0.5 * wc_ref[j], LRU_HEAD_DIM, LRU_HEAD_DIM,
                                         MXU_DIM // LRU_HEAD_DIM)
        s5st[...] = jnp.zeros_like(s5st)
        lrust[...] = jnp.zeros_like(lrust)
        ubuf[pl.ds(0, hdr), :] = jnp.zeros((hdr, lw), F32)

    for b in range(nb):
        in_copy(step, slot, b).wait()
    xt = xbuf[slot].reshape(m, d)
    h = _rms(xt, v("g_mix")).astype(BF16)
    neg_sp = -LRU_C * _softplus(-v("lru_lambda"))
    g0 = s5w + lw
    n_blk = lw // MXU_DIM
    gate_w = 2 * d // n_blk

    def in_proj(lo, width, bias_scale=1.0):
        return _dot(h, win[:, lo:lo + width]) + bias_scale * v("b_in", lo, width)

    u = in_proj(0, s5w)
    ua[...] = u
    ub16 = u.astype(BF16)
    ub_all = in_proj(s5w, lw)
    ubuf[pl.ds(hdr, m), :] = ub_all

    def lru_block(j):
        cs = slice(j * MXU_DIM, (j + 1) * MXU_DIM)
        ub = ub_all[:, cs]
        xc = (v("conv_b", j * MXU_DIM, MXU_DIM)
              + v("conv_w", j * MXU_DIM, MXU_DIM, row=CONV_WIDTH - 1) * ub)
        for k in range(CONV_WIDTH - 1):
            xc = xc + (v("conv_w", j * MXU_DIM, MXU_DIM, row=k)
                       * ubuf[pl.ds(k * nb, m), cs])
        if j < 2:
            pr = _dot(ub16[:, j * half_w:(j + 1) * half_w], bbd_ref[j])
            xr[:, j * half_st:(j + 1) * half_st] = pr[:, :half_st]
            xi[:, j * half_st:(j + 1) * half_st] = pr[:, half_st:]
        g = _dot(xc.astype(BF16), wg_ref[j])
        r = _sigmoid_of_twice(g[:, :MXU_DIM] + 0.5 * v("b_r", j * MXU_DIM, MXU_DIM))
        ig = _sigmoid_of_twice(g[:, MXU_DIM:] + 0.5 * v("b_i", j * MXU_DIM, MXU_DIM))
        a = jnp.exp(r * neg_sp[:, cs])
        abuf[:, cs] = a
        bxbuf[:, cs] = _sqrt_nonneg(1.0 - a * a) * ig * xc
        gates[:, j * gate_w:(j + 1) * gate_w] = _sigmoid_of_twice(
            in_proj(g0 + j * gate_w, gate_w, bias_scale=0.5)).astype(BF16)
    n_pass = nst // SCAN_LANES
    lru_lanes = lw // n_pass

    def scan_pass(c):
        ls = slice(c * SCAN_LANES, (c + 1) * SCAN_LANES)
        ll = slice(c * lru_lanes, (c + 1) * lru_lanes)
        ar = jnp.broadcast_to(v("a_re", c * SCAN_LANES, SCAN_LANES), (nb, SCAN_LANES))
        ai = jnp.broadcast_to(v("a_im", c * SCAN_LANES, SCAN_LANES), (nb, SCAN_LANES))

        def scan_step(t, carry):
            sr, si, hl = carry
            rows = pl.ds(pl.multiple_of(t * nb, nb), nb)
            nsr = ar * sr - ai * si + xr[rows, ls]
            nsi = ar * si + ai * sr + xi[rows, ls]
            xr[rows, ls] = nsr
            xi[rows, ls] = nsi
            hn = abuf[rows, ll] * hl + bxbuf[rows, ll]
            bxbuf[rows, ll] = hn
            return nsr, nsi, hn

        sr, si, hl = lax.fori_loop(
            0, tq, scan_step, (s5st[0, :, ls], s5st[1, :, ls], lrust[:, ll]),
            unroll=SCAN_UNROLL)
        s5st[0, :, ls] = sr
        s5st[1, :, ls] = si
        lrust[:, ll] = hl

    def s5_readout(hh):
        ls = slice(hh * half_st, (hh + 1) * half_st)
        return (_dot(xr[:, ls].astype(BF16), ccat_ref[hh, 0])
                + _dot(xi[:, ls].astype(BF16), ccat_ref[hh, 1]))

    for j in range(n_blk):
        lru_block(j)
    ubuf[pl.ds(0, hdr), :] = ubuf[pl.ds(m, hdr), :]
    for c in range(n_pass):
        scan_pass(c)
    y0 = s5_readout(0)
    y1 = s5_readout(1)

    y = jnp.concatenate([y0, y1], axis=1) + v("s5_d") * ua[...]
    z = jax.nn.gelu(y)
    y_a = z * _sigmoid_of_twice(_dot(z.astype(BF16), wglu[...]) + 0.5 * v("b_glu"))
    merged = gates[:, 0:d] * _dot(y_a.astype(BF16), waout[...])
    merged = merged + gates[:, d:2 * d] * _dot(bxbuf[...].astype(BF16), wbout[...])
    x1 = xbuf[slot].reshape(m, d) + _dot(merged.astype(BF16), wo[...])
    obuf[slot] = x1.reshape(tq, nb, d)
    for b in range(nb):
        out_copy(step, slot, b).start()

    @pl.when(step == n_steps - 1)
    def _():
        for b in range(nb):
            out_copy(step, slot, b).wait()
        for b in range(nb):
            out_copy(step - 1, 1 - slot, b).wait()


def _const_spec(shape):
    nd = len(shape)
    return pl.BlockSpec(shape, lambda i, _nd=nd: (0,) * _nd,
                        pipeline_mode=pl.Buffered(1))


def _mixer_call(x, vec, dense, bc, cc, wc):
    nb, seq, d = x.shape
    m = nb * TQ
    w_in, w_glu, w_a_out, w_b_out, w_o = dense
    s5w = w_glu.shape[1]
    n_half, half_w, _ = bc.shape
    half_st = half_w // S5_GROUP_CH * cc.shape[2]
    nst = n_half * half_st
    lw = w_b_out.shape[1]
    hbm_spec = pl.BlockSpec(memory_space=pl.ANY)
    in_specs = [hbm_spec] + [_const_spec(a.shape) for a in vec.arrays]
    in_specs += [pl.BlockSpec(memory_space=pl.ANY) for _ in dense]
    in_specs += [_const_spec(c.shape) for c in (bc, cc, wc)]
    scratch = [pltpu.VMEM(w.shape[1:], BF16) for w in dense]
    scratch += [
        pltpu.VMEM((STAGE_SLOTS,) + MIXER_STAGE, F32),
        pltpu.SemaphoreType.DMA((STAGE_SLOTS,)),
        pltpu.VMEM((n_half, half_w, 2 * half_st), BF16),
        pltpu.VMEM((n_half, 2, half_st, half_w), BF16),
        pltpu.VMEM((wc.shape[0], MXU_DIM, 2 * MXU_DIM), BF16),
        pltpu.VMEM((2, TQ, nb, d), F32),
        pltpu.VMEM((2, TQ, nb, d), F32),
        pltpu.SemaphoreType.DMA((2, nb)),
        pltpu.SemaphoreType.DMA((2, nb)),
        pltpu.VMEM((m, 2 * d), BF16),
        pltpu.VMEM((m, s5w), F32),
        pltpu.VMEM((m + (CONV_WIDTH - 1) * nb, lw), F32),
        pltpu.VMEM((m, nst), F32),
        pltpu.VMEM((m, nst), F32),
        pltpu.VMEM((2, nb, nst), F32),
        pltpu.VMEM((m, lw), F32),
        pltpu.VMEM((m, lw), F32),
        pltpu.VMEM((nb, lw), F32),
    ]
    return pl.pallas_call(
        functools.partial(_mixer_kernel, vec),
        out_shape=jax.ShapeDtypeStruct(x.shape, F32),
        grid=(seq // TQ,),
        in_specs=in_specs,
        out_specs=hbm_spec,
        scratch_shapes=scratch,
        compiler_params=pltpu.CompilerParams(
            dimension_semantics=("arbitrary",), vmem_limit_bytes=VMEM_LIMIT),
        name="mixer",
    )(x, *vec.arrays, *dense, bc, cc, wc)


def _ffn_kernel(vec, x_ref, p_ref, *refs):
    n_vec = len(vec.names)
    (wgate_hbm, wup_hbm, wdown_hbm, wpg_hbm, wple_hbm, o_ref,
     wgate, wup, wdown, wpg, wple, stage, sem) = refs[n_vec:]
    v = vec.view(refs[:n_vec])

    @pl.when(pl.program_id(0) == 0)
    def _():
        _stream_cast([(wgate_hbm.at[0], wgate, None), (wup_hbm.at[0], wup, None),
                      (wdown_hbm.at[0], wdown, None), (wpg_hbm.at[0], wpg, None),
                      (wple_hbm.at[0], wple, None)], stage, sem)

    hid = wgate.shape[1]
    for s in range(FFN_ROWS // FFN_SUB_ROWS):
        rows = pl.ds(s * FFN_SUB_ROWS, FFN_SUB_ROWS)
        x = x_ref[rows, :]
        h2 = _rms(x, v("g_ffn")).astype(BF16)
        acc = x
        for j in range(hid // FFN_HID_BLK):
            cs = slice(j * FFN_HID_BLK, (j + 1) * FFN_HID_BLK)
            gt = _dot(h2, wgate[:, cs])
            a = gt * _sigmoid(gt) * _dot(h2, wup[:, cs])
            acc = acc + _dot(a.astype(BF16), wdown[cs, :])
        x2 = acc
        gate_p = _sigmoid(
            _dot(_rms(x2, v("g_ple_gate")).astype(BF16), wpg[...]) + v("b_ple_gate"))
        e = _rms(_dot(p_ref[rows, :].astype(BF16), wple[...]), v("g_ple"))
        x3 = x2 + gate_p * e
        o_ref[rows, :] = _rms(x3, v("g_final"))


def _ffn_call(x1, p, vec, dense):
    bsz, seq, d = x1.shape
    pd = p.shape[-1]
    per_seq = seq // FFN_ROWS
    in_specs = [
        pl.BlockSpec((None, FFN_ROWS, d), lambda i: (i // per_seq, i % per_seq, 0)),
        pl.BlockSpec((None, None, FFN_ROWS, pd),
                     lambda i: (0, i // per_seq, i % per_seq, 0))]
    in_specs += [_const_spec(a.shape) for a in vec.arrays]
    in_specs += [pl.BlockSpec(memory_space=pl.ANY) for _ in dense]
    scratch = [pltpu.VMEM(w.shape[1:], BF16) for w in dense]
    scratch += [pltpu.VMEM((STAGE_SLOTS,) + FFN_STAGE, F32),
                pltpu.SemaphoreType.DMA((STAGE_SLOTS,))]
    return pl.pallas_call(
        functools.partial(_ffn_kernel, vec),
        out_shape=jax.ShapeDtypeStruct((bsz, seq, d), F32),
        grid=(bsz * per_seq,),
        in_specs=in_specs,
        out_specs=pl.BlockSpec((None, FFN_ROWS, d),
                               lambda i: (i // per_seq, i % per_seq, 0)),
        scratch_shapes=scratch,
        compiler_params=pltpu.CompilerParams(
            dimension_semantics=("arbitrary",), vmem_limit_bytes=VMEM_LIMIT),
        name="ffn",
    )(x1, p, *vec.arrays, *dense)


def _s5_params(lam_re, lam_im, log_dt, b_re, b_im, c_re, c_im):
    g, n = lam_re.shape
    p = b_re.shape[2]
    hg = g // 2
    dt = jnp.exp(log_dt)[:, None]
    mag = jnp.exp(lam_re * dt)
    ar = mag * jnp.cos(lam_im * dt)
    ai = mag * jnp.sin(lam_im * dt)
    den = lam_re * lam_re + lam_im * lam_im
    nr = ar - 1.0
    fr = (nr * lam_re + ai * lam_im) / den
    fi = (ai * lam_re - nr * lam_im) / den
    bbr = fr[..., None] * b_re - fi[..., None] * b_im
    bbi = fr[..., None] * b_im + fi[..., None] * b_re
    bc = jnp.stack([bbr, bbi]).reshape(2, 2, hg, n, p).transpose(1, 2, 4, 0, 3)
    bc = bc.reshape(2, hg * p, 2 * n)
    cc = jnp.stack([c_re, -c_im]).reshape(2, 2, hg, p, n).transpose(1, 0, 4, 2, 3)
    cc = cc.reshape(2, 2, n, hg * p)
    return ar.reshape(-1), ai.reshape(-1), bc, cc


def _lru_gate_weights(w_r, w_i):
    heads, hd, _ = w_r.shape
    per = MXU_DIM // hd
    tiles = heads // per
    w = jnp.stack([w_r, w_i]).reshape(2, tiles, per, hd, hd).transpose(1, 2, 3, 0, 4)
    return w.reshape(tiles, per * hd, 2 * hd)


def kernel(x, p, g_mix, w_in, b_in, lam_re, lam_im, log_dt, s5_b_re, s5_b_im, s5_c_re, s5_c_im, s5_d, w_glu, b_glu, conv_w, conv_b, w_r, b_r, w_i, b_i, lru_lambda, w_a_out, w_b_out, w_o, g_ffn, w_ffn_gate, w_ffn_up, w_ffn_down, g_ple_gate, w_ple_gate, b_ple_gate, w_ple, g_ple, g_final):
    bsz, seq, d = x.shape
    assert w_in.shape[0] == 1
    assert bsz == SUBLANES and seq % TQ == 0 and seq % FFN_ROWS == 0

    a_re, a_im, bc, cc = _s5_params(
        lam_re[0], lam_im[0], log_dt[0], s5_b_re[0], s5_b_im[0],
        s5_c_re[0], s5_c_im[0])
    def row(a):
        return a.reshape(1, -1)

    mixer_vec = _Vectors([
        ("g_mix", g_mix), ("b_in", b_in), ("a_re", row(a_re)), ("a_im", row(a_im)),
        ("s5_d", row(s5_d)), ("b_glu", b_glu), ("conv_w", conv_w[0]),
        ("conv_b", conv_b), ("b_r", row(b_r)), ("b_i", row(b_i)),
        ("lru_lambda", lru_lambda)])
    x1 = _mixer_call(x, mixer_vec, (w_in, w_glu, w_a_out, w_b_out, w_o),
                     bc, cc, _lru_gate_weights(w_r[0], w_i[0]))
    ffn_vec = _Vectors([
        ("g_ffn", g_ffn), ("g_ple_gate", g_ple_gate), ("b_ple_gate", b_ple_gate),
        ("g_ple", g_ple), ("g_final", row(g_final))])
    return _ffn_call(x1, p, ffn_vec,
                     (w_ffn_gate, w_ffn_up, w_ffn_down, w_ple_gate, w_ple))
```

```python
import functools

import jax
import jax.numpy as jnp
from jax import lax
from jax.experimental import pallas as pl
from jax.experimental.pallas import tpu as pltpu

F32 = jnp.float32
BF16 = jnp.bfloat16

EPS = 1e-6
LRU_C = 8.0
CONV_WIDTH = 4
LRU_HEAD_DIM = 64
S5_GROUP_CH = 16

SUBLANES = 8
LANES = 128
MXU_DIM = 256
TQ = 64
SCAN_LANES = 512
STAGE_SLOTS = 6
MIXER_STAGE = (256, 1024)
FFN_ROWS = 1024
FFN_SUB_ROWS = 512
FFN_HID_BLK = 256
FFN_STAGE = (256, 1408)
VMEM_LIMIT = 56 * 1024 * 1024


def _rms(x, g):
    return x * lax.rsqrt(jnp.mean(x * x, axis=-1, keepdims=True) + EPS) * g


def _dot(a, b):
    return jnp.dot(a, b, preferred_element_type=F32)


def _sigmoid_of_twice(half_x):
    return 0.5 * jnp.tanh(half_x) + 0.5


def _sigmoid(x):
    return _sigmoid_of_twice(0.5 * x)


def _sqrt_nonneg(v):
    return jnp.where(v > 0.0, v * lax.rsqrt(v), 0.0)


def _softplus(x):
    return jnp.maximum(x, 0.0) + jnp.log1p(jnp.exp(-jnp.abs(x)))


class _Vectors:
    def __init__(self, named):
        self.names = [name for name, _ in named]
        self.arrays = [a.astype(F32) for _, a in named]
        assert all(a.ndim == 2 and a.shape[1] % LANES == 0 for a in self.arrays)

    def view(self, refs):
        index = {name: i for i, name in enumerate(self.names)}

        def get(name, lo=0, width=None, row=0):
            ref = refs[index[name]]
            width = ref.shape[1] - lo if width is None else width
            return ref[row:row + 1, lo:lo + width]
        return get


def _stream_cast(jobs, stage, sem):
    n_slots, stage_rows, stage_cols = stage.shape
    chunks = []
    for src, dst, scale in jobs:
        rows, cols = dst.shape
        rc = min(rows, stage_rows)
        assert rows % rc == 0 and cols % LANES == 0
        for r0 in range(0, rows, rc):
            for c0 in range(0, cols, stage_cols):
                chunks.append((src, dst, r0, rc, c0, min(stage_cols, cols - c0), scale))

    def copy(i):
        src, _, r0, rc, c0, cw, _ = chunks[i]
        slot = i % n_slots
        return pltpu.make_async_copy(
            src.at[pl.ds(r0, rc), pl.ds(c0, cw)],
            stage.at[slot, pl.ds(0, rc), pl.ds(0, cw)],
            sem.at[slot])

    ahead = n_slots - 1
    for i in range(min(ahead, len(chunks))):
        copy(i).start()
    for i in range(len(chunks)):
        if i + ahead < len(chunks):
            copy(i + ahead).start()
        copy(i).wait()
        _, dst, r0, rc, c0, cw, scale = chunks[i]
        w = stage[i % n_slots, pl.ds(0, rc), pl.ds(0, cw)]
        if scale is not None:
            w = w * scale[:, c0:c0 + cw]
        dst[pl.ds(r0, rc), pl.ds(c0, cw)] = w.astype(BF16)


def _pow2_div(x, n):
    assert n & (n - 1) == 0
    return x >> (n.bit_length() - 1)


def _pow2_mod(x, n):
    assert n & (n - 1) == 0
    return x & (n - 1)


def _block_diag_cols(compact, row_blk, col_blk, n_blocks):
    rows, ccols = compact.shape
    n_parts = ccols // col_blk
    part_w = n_blocks * col_blk
    ocols = n_parts * part_w
    k_i = lax.broadcasted_iota(jnp.int32, (ccols, ocols), 0)
    c_i = lax.broadcasted_iota(jnp.int32, (ccols, ocols), 1)
    onehot = ((_pow2_div(k_i, col_blk) == _pow2_div(c_i, part_w))
              & (_pow2_mod(k_i, col_blk) == _pow2_mod(c_i, col_blk)))
    tiled = _dot(compact.astype(BF16), onehot.astype(BF16))
    r_o = lax.broadcasted_iota(jnp.int32, (rows, ocols), 0)
    c_o = lax.broadcasted_iota(jnp.int32, (rows, ocols), 1)
    keep = _pow2_div(r_o, row_blk) == _pow2_div(_pow2_mod(c_o, part_w), col_blk)
    return jnp.where(keep, tiled, 0.0).astype(BF16)


def _block_diag_rows(compact, row_blk, col_blk, n_blocks):
    tiled = jnp.concatenate([compact] * n_blocks, axis=0)
    r_o = lax.broadcasted_iota(jnp.int32, tiled.shape, 0)
    c_o = lax.broadcasted_iota(jnp.int32, tiled.shape, 1)
    keep = _pow2_div(r_o, row_blk) == _pow2_div(c_o, col_blk)
    return jnp.where(keep, tiled, 0.0).astype(BF16)


def _mixer_kernel(vec, x_hbm, *refs):
    n_vec = len(vec.names)
    (win_hbm, wglu_hbm, waout_hbm, wbout_hbm, wo_hbm, bc_ref, cc_ref, wc_ref, o_hbm,
     win, wglu, waout, wbout, wo, stage, sem, bbd_ref, ccat_ref, wg_ref,
     xbuf, obuf, in_sem, out_sem,
     gates, ua, ubuf, xr, xi, s5st, abuf, bxbuf, lrust) = refs[n_vec:]
    _, tq, nb, d = xbuf.shape
    m = nb * tq
    s5w = ua.shape[1]
    nst = xr.shape[1]
    half_w = s5w // 2
    half_st = nst // 2
    lw = abuf.shape[1]
    hdr = (CONV_WIDTH - 1) * nb
    v = vec.view(refs[:n_vec])
    step = pl.program_id(0)
    n_steps = pl.num_programs(0)
    slot = step % 2

    def in_copy(tile, slot_, b):
        return pltpu.make_async_copy(
            x_hbm.at[b, pl.ds(tile * tq, tq), :], xbuf.at[slot_, :, b, :],
            in_sem.at[slot_, b])

    def out_copy(tile, slot_, b):
        return pltpu.make_async_copy(
            obuf.at[slot_, :, b, :], o_hbm.at[b, pl.ds(tile * tq, tq), :],
            out_sem.at[slot_, b])

    @pl.when(step == 0)
    def _():
        for b in range(nb):
            in_copy(0, 0, b).start()

    @pl.when(step + 1 < n_steps)
    def _():
        for b in range(nb):
            in_copy(step + 1, 1 - slot, b).start()

    @pl.when(step >= 2)
    def _():
        for b in range(nb):
            out_copy(step - 2, slot, b).wait()

    @pl.when(step == 0)
    def _():
        col = lax.broadcasted_iota(jnp.int32, (1, win.shape[1]), 1)
        in_scale = jnp.where(col >= s5w + lw, 0.5, 1.0)
        glu_scale = jnp.full((1, wglu.shape[1]), 0.5, F32)
        _stream_cast([(win_hbm.at[0], win, in_scale),
                      (wglu_hbm.at[0], wglu, glu_scale),
                      (waout_hbm.at[0], waout, None), (wbout_hbm.at[0], wbout, None),
                      (wo_hbm.at[0], wo, None)], stage, sem)
        s5_p = S5_GROUP_CH
        s5_n = cc_ref.shape[2]
        n_grp = bc_ref.shape[1] // s5_p
        for hh in range(bbd_ref.shape[0]):
            bbd_ref[hh] = _block_diag_cols(bc_ref[hh], s5_p, s5_n, n_grp)
            for ri in range(2):
                ccat_ref[hh, ri] = _block_diag_rows(cc_ref[hh, ri], s5_n, s5_p, n_grp)
        for j in range(wg_ref.shape[0]):
            wg_ref[j] = _block_diag_cols(0.5 * wc_ref[j], LRU_HEAD_DIM, LRU_HEAD_DIM,
                                         MXU_DIM // LRU_HEAD_DIM)
        s5st[...] = jnp.zeros_like(s5st)
        lrust[...] = jnp.zeros_like(lrust)
        ubuf[pl.ds(0, hdr), :] = jnp.zeros((hdr, lw), F32)

    for b in range(nb):
        in_copy(step, slot, b).wait()
    xt = xbuf[slot].reshape(m, d)
    h = _rms(xt, v("g_mix")).astype(BF16)
    neg_sp = -LRU_C * _softplus(-v("lru_lambda"))
    g0 = s5w + lw
    n_blk = lw // MXU_DIM
    gate_w = 2 * d // n_blk

    def in_proj(lo, width, bias_scale=1.0):
        return _dot(h, win[:, lo:lo + width]) + bias_scale * v("b_in", lo, width)

    u = in_proj(0, s5w)
    ua[...] = u
    ub16 = u.astype(BF16)
    ub_all = in_proj(s5w, lw)
    ubuf[pl.ds(hdr, m), :] = ub_all

    def lru_block(j):
        cs = slice(j * MXU_DIM, (j + 1) * MXU_DIM)
        ub = ub_all[:, cs]
        xc = (v("conv_b", j * MXU_DIM, MXU_DIM)
              + v("conv_w", j * MXU_DIM, MXU_DIM, row=CONV_WIDTH - 1) * ub)
        for k in range(CONV_WIDTH - 1):
            xc = xc + (v("conv_w", j * MXU_DIM, MXU_DIM, row=k)
                       * ubuf[pl.ds(k * nb, m), cs])
        if j < bbd_ref.shape[0]:
            pr = _dot(ub16[:, j * half_w:(j + 1) * half_w], bbd_ref[j])
            xr[:, j * half_st:(j + 1) * half_st] = pr[:, :half_st]
            xi[:, j * half_st:(j + 1) * half_st] = pr[:, half_st:]
        g = _dot(xc.astype(BF16), wg_ref[j])
        r = _sigmoid_of_twice(g[:, :MXU_DIM] + 0.5 * v("b_r", j * MXU_DIM, MXU_DIM))
        ig = _sigmoid_of_twice(g[:, MXU_DIM:] + 0.5 * v("b_i", j * MXU_DIM, MXU_DIM))
        a = jnp.exp(r * neg_sp[:, cs])
        abuf[:, cs] = a
        bxbuf[:, cs] = _sqrt_nonneg(1.0 - a * a) * ig * xc
        gates[:, j * gate_w:(j + 1) * gate_w] = _sigmoid_of_twice(
            in_proj(g0 + j * gate_w, gate_w, bias_scale=0.5)).astype(BF16)
    n_pass = nst // SCAN_LANES
    lru_lanes = lw // n_pass

    def scan_pass(c):
        ls = slice(c * SCAN_LANES, (c + 1) * SCAN_LANES)
        ll = slice(c * lru_lanes, (c + 1) * lru_lanes)
        ar = jnp.broadcast_to(v("a_re", c * SCAN_LANES, SCAN_LANES), (nb, SCAN_LANES))
        ai = jnp.broadcast_to(v("a_im", c * SCAN_LANES, SCAN_LANES), (nb, SCAN_LANES))

        def scan_step(t, carry):
            sr, si, hl = carry
            rows = pl.ds(pl.multiple_of(t * nb, nb), nb)
            nsr = ar * sr - ai * si + xr[rows, ls]
            nsi = ar * si + ai * sr + xi[rows, ls]
            xr[rows, ls] = nsr
            xi[rows, ls] = nsi
            hn = abuf[rows, ll] * hl + bxbuf[rows, ll]
            bxbuf[rows, ll] = hn
            return nsr, nsi, hn

        sr, si, hl = lax.fori_loop(
            0, tq, scan_step, (s5st[0, :, ls], s5st[1, :, ls], lrust[:, ll]),
            unroll=tq)
        s5st[0, :, ls] = sr
        s5st[1, :, ls] = si
        lrust[:, ll] = hl

    def s5_readout(hh):
        ls = slice(hh * half_st, (hh + 1) * half_st)
        return (_dot(xr[:, ls].astype(BF16), ccat_ref[hh, 0])
                + _dot(xi[:, ls].astype(BF16), ccat_ref[hh, 1]))

    for j in range(n_blk):
        lru_block(j)
    ubuf[pl.ds(0, hdr), :] = ubuf[pl.ds(m, hdr), :]
    for c in range(n_pass):
        scan_pass(c)
    y0 = s5_readout(0)
    y1 = s5_readout(1)

    y = jnp.concatenate([y0, y1], axis=1) + v("s5_d") * ua[...]
    z = jax.nn.gelu(y)
    y_a = z * _sigmoid_of_twice(_dot(z.astype(BF16), wglu[...]) + 0.5 * v("b_glu"))
    merged = gates[:, 0:d] * _dot(y_a.astype(BF16), waout[...])
    merged = merged + gates[:, d:2 * d] * _dot(bxbuf[...].astype(BF16), wbout[...])
    x1 = xbuf[slot].reshape(m, d) + _dot(merged.astype(BF16), wo[...])
    obuf[slot] = x1.reshape(tq, nb, d)
    for b in range(nb):
        out_copy(step, slot, b).start()

    @pl.when(step == n_steps - 1)
    def _():
        for b in range(nb):
            out_copy(step, slot, b).wait()
        for b in range(nb):
            out_copy(step - 1, 1 - slot, b).wait()


def _const_spec(shape):
    nd = len(shape)
    return pl.BlockSpec(shape, lambda i, _nd=nd: (0,) * _nd,
                        pipeline_mode=pl.Buffered(1))


def _mixer_call(x, vec, dense, bc, cc, wc):
    nb, seq, d = x.shape
    m = nb * TQ
    w_in, w_glu, w_a_out, w_b_out, w_o = dense
    s5w = w_glu.shape[1]
    n_half, half_w, _ = bc.shape
    half_st = half_w // S5_GROUP_CH * cc.shape[2]
    nst = n_half * half_st
    lw = w_b_out.shape[1]
    hbm_spec = pl.BlockSpec(memory_space=pl.ANY)
    in_specs = [hbm_spec] + [_const_spec(a.shape) for a in vec.arrays]
    in_specs += [pl.BlockSpec(memory_space=pl.ANY) for _ in dense]
    in_specs += [_const_spec(c.shape) for c in (bc, cc, wc)]
    scratch = [pltpu.VMEM(w.shape[1:], BF16) for w in dense]
    scratch += [
        pltpu.VMEM((STAGE_SLOTS,) + MIXER_STAGE, F32),
        pltpu.SemaphoreType.DMA((STAGE_SLOTS,)),
        pltpu.VMEM((n_half, half_w, 2 * half_st), BF16),
        pltpu.VMEM((n_half, 2, half_st, half_w), BF16),
        pltpu.VMEM((wc.shape[0], MXU_DIM, 2 * MXU_DIM), BF16),
        pltpu.VMEM((2, TQ, nb, d), F32),
        pltpu.VMEM((2, TQ, nb, d), F32),
        pltpu.SemaphoreType.DMA((2, nb)),
        pltpu.SemaphoreType.DMA((2, nb)),
        pltpu.VMEM((m, 2 * d), BF16),
        pltpu.VMEM((m, s5w), F32),
        pltpu.VMEM((m + (CONV_WIDTH - 1) * nb, lw), F32),
        pltpu.VMEM((m, nst), F32),
        pltpu.VMEM((m, nst), F32),
        pltpu.VMEM((2, nb, nst), F32),
        pltpu.VMEM((m, lw), F32),
        pltpu.VMEM((m, lw), F32),
        pltpu.VMEM((nb, lw), F32),
    ]
    return pl.pallas_call(
        functools.partial(_mixer_kernel, vec),
        out_shape=jax.ShapeDtypeStruct(x.shape, F32),
        grid=(seq // TQ,),
        in_specs=in_specs,
        out_specs=hbm_spec,
        scratch_shapes=scratch,
        compiler_params=pltpu.CompilerParams(
            dimension_semantics=("arbitrary",), vmem_limit_bytes=VMEM_LIMIT),
        name="mixer",
    )(x, *vec.arrays, *dense, bc, cc, wc)


def _ffn_kernel(vec, x_ref, p_ref, *refs):
    n_vec = len(vec.names)
    (wgate_hbm, wup_hbm, wdown_hbm, wpg_hbm, wple_hbm, o_ref,
     wgate, wup, wdown, wpg, wple, stage, sem) = refs[n_vec:]
    v = vec.view(refs[:n_vec])

    @pl.when(pl.program_id(0) == 0)
    def _():
        _stream_cast([(wgate_hbm.at[0], wgate, None), (wup_hbm.at[0], wup, None),
                      (wdown_hbm.at[0], wdown, None), (wpg_hbm.at[0], wpg, None),
                      (wple_hbm.at[0], wple, None)], stage, sem)

    hid = wgate.shape[1]
    for s in range(FFN_ROWS // FFN_SUB_ROWS):
        rows = pl.ds(s * FFN_SUB_ROWS, FFN_SUB_ROWS)
        x = x_ref[rows, :]
        h2 = _rms(x, v("g_ffn")).astype(BF16)
        acc = x
        for j in range(hid // FFN_HID_BLK):
            cs = slice(j * FFN_HID_BLK, (j + 1) * FFN_HID_BLK)
            gt = _dot(h2, wgate[:, cs])
            a = gt * _sigmoid(gt) * _dot(h2, wup[:, cs])
            acc = acc + _dot(a.astype(BF16), wdown[cs, :])
        x2 = acc
        gate_p = _sigmoid(
            _dot(_rms(x2, v("g_ple_gate")).astype(BF16), wpg[...]) + v("b_ple_gate"))
        e = _rms(_dot(p_ref[rows, :].astype(BF16), wple[...]), v("g_ple"))
        x3 = x2 + gate_p * e
        o_ref[rows, :] = _rms(x3, v("g_final"))


def _ffn_call(x1, p, vec, dense):
    bsz, seq, d = x1.shape
    pd = p.shape[-1]
    per_seq = seq // FFN_ROWS
    in_specs = [
        pl.BlockSpec((None, FFN_ROWS, d), lambda i: (i // per_seq, i % per_seq, 0)),
        pl.BlockSpec((None, None, FFN_ROWS, pd),
                     lambda i: (0, i // per_seq, i % per_seq, 0))]
    in_specs += [_const_spec(a.shape) for a in vec.arrays]
    in_specs += [pl.BlockSpec(memory_space=pl.ANY) for _ in dense]
    scratch = [pltpu.VMEM(w.shape[1:], BF16) for w in dense]
    scratch += [pltpu.VMEM((STAGE_SLOTS,) + FFN_STAGE, F32),
                pltpu.SemaphoreType.DMA((STAGE_SLOTS,))]
    return pl.pallas_call(
        functools.partial(_ffn_kernel, vec),
        out_shape=jax.ShapeDtypeStruct((bsz, seq, d), F32),
        grid=(bsz * per_seq,),
        in_specs=in_specs,
        out_specs=pl.BlockSpec((None, FFN_ROWS, d),
                               lambda i: (i // per_seq, i % per_seq, 0)),
        scratch_shapes=scratch,
        compiler_params=pltpu.CompilerParams(
            dimension_semantics=("arbitrary",), vmem_limit_bytes=VMEM_LIMIT),
        name="ffn",
    )(x1, p, *vec.arrays, *dense)


def _s5_params(lam_re, lam_im, log_dt, b_re, b_im, c_re, c_im):
    g, n = lam_re.shape
    p = b_re.shape[2]
    hg = g // 2
    dt = jnp.exp(log_dt)[:, None]
    mag = jnp.exp(lam_re * dt)
    ar = mag * jnp.cos(lam_im * dt)
    ai = mag * jnp.sin(lam_im * dt)
    den = lam_re * lam_re + lam_im * lam_im
    nr = ar - 1.0
    fr = (nr * lam_re + ai * lam_im) / den
    fi = (ai * lam_re - nr * lam_im) / den
    bbr = fr[..., None] * b_re - fi[..., None] * b_im
    bbi = fr[..., None] * b_im + fi[..., None] * b_re
    bc = jnp.stack([bbr, bbi]).reshape(2, 2, hg, n, p).transpose(1, 2, 4, 0, 3)
    bc = bc.reshape(2, hg * p, 2 * n)
    cc = jnp.stack([c_re, -c_im]).reshape(2, 2, hg, p, n).transpose(1, 0, 4, 2, 3)
    cc = cc.reshape(2, 2, n, hg * p)
    return ar.reshape(-1), ai.reshape(-1), bc, cc


def _lru_gate_weights(w_r, w_i):
    heads, hd, _ = w_r.shape
    per = MXU_DIM // hd
    tiles = heads // per
    w = jnp.stack([w_r, w_i]).reshape(2, tiles, per, hd, hd).transpose(1, 2, 3, 0, 4)
    return w.reshape(tiles, per * hd, 2 * hd)


def kernel(x, p, g_mix, w_in, b_in, lam_re, lam_im, log_dt, s5_b_re, s5_b_im, s5_c_re, s5_c_im, s5_d, w_glu, b_glu, conv_w, conv_b, w_r, b_r, w_i, b_i, lru_lambda, w_a_out, w_b_out, w_o, g_ffn, w_ffn_gate, w_ffn_up, w_ffn_down, g_ple_gate, w_ple_gate, b_ple_gate, w_ple, g_ple, g_final):
    bsz, seq, d = x.shape
    assert w_in.shape[0] == 1
    assert bsz == SUBLANES and seq % TQ == 0 and seq // TQ >= 2
    assert seq % FFN_ROWS == 0

    a_re, a_im, bc, cc = _s5_params(
        lam_re[0], lam_im[0], log_dt[0], s5_b_re[0], s5_b_im[0],
        s5_c_re[0], s5_c_im[0])
    def row(a):
        return a.reshape(1, -1)

    mixer_vec = _Vectors([
        ("g_mix", g_mix), ("b_in", b_in), ("a_re", row(a_re)), ("a_im", row(a_im)),
        ("s5_d", row(s5_d)), ("b_glu", b_glu), ("conv_w", conv_w[0]),
        ("conv_b", conv_b), ("b_r", row(b_r)), ("b_i", row(b_i)),
        ("lru_lambda", lru_lambda)])
    x1 = _mixer_call(x, mixer_vec, (w_in, w_glu, w_a_out, w_b_out, w_o),
                     bc, cc, _lru_gate_weights(w_r[0], w_i[0]))
    ffn_vec = _Vectors([
        ("g_ffn", g_ffn), ("g_ple_gate", g_ple_gate), ("b_ple_gate", b_ple_gate),
        ("g_ple", g_ple), ("g_final", row(g_final))])
    return _ffn_call(x1, p, ffn_vec,
                     (w_ffn_gate, w_ffn_up, w_ffn_down, w_ple_gate, w_ple))
```

```python
import functools

import jax
import jax.numpy as jnp
from jax import lax
from jax.experimental import pallas as pl
from jax.experimental.pallas import tpu as pltpu

F32 = jnp.float32
BF16 = jnp.bfloat16

EPS = 1e-6
LRU_C = 8.0
CONV_WIDTH = 4
LRU_HEAD_DIM = 64
S5_GROUP_CH = 16

SUBLANES = 8
LANES = 128
MXU_DIM = 256
TQ = 64
SCAN_LANES = 512
STAGE_SLOTS = 6
MIXER_STAGE = (256, 1024)
FFN_ROWS = 1024
FFN_SUB_ROWS = 512
FFN_HID_BLK = 256
FFN_STAGE = (256, 1408)
VMEM_LIMIT = 56 * 1024 * 1024


def _rms(x, g):
    return x * lax.rsqrt(jnp.mean(x * x, axis=-1, keepdims=True) + EPS) * g


def _dot(a, b):
    return jnp.dot(a, b, preferred_element_type=F32)


def _sigmoid_of_twice(half_x):
    return 0.5 * jnp.tanh(half_x) + 0.5


def _sigmoid(x):
    return _sigmoid_of_twice(0.5 * x)


def _sqrt_nonneg(v):
    return jnp.where(v > 0.0, v * lax.rsqrt(v), 0.0)


def _softplus(x):
    return jnp.maximum(x, 0.0) + jnp.log1p(jnp.exp(-jnp.abs(x)))


class _Vectors:
    def __init__(self, named):
        self.names = [name for name, _ in named]
        self.arrays = [a.astype(F32) for _, a in named]
        assert all(a.ndim == 2 and a.shape[1] % LANES == 0 for a in self.arrays)

    def view(self, refs):
        index = {name: i for i, name in enumerate(self.names)}

        def get(name, lo=0, width=None, row=0):
            ref = refs[index[name]]
            width = ref.shape[1] - lo if width is None else width
            return ref[row:row + 1, lo:lo + width]
        return get


def _stream_cast(jobs, stage, sem):
    n_slots, stage_rows, stage_cols = stage.shape
    chunks = []
    for src, dst, scale in jobs:
        rows, cols = dst.shape
        rc = min(rows, stage_rows)
        assert rows % rc == 0 and cols % LANES == 0
        for r0 in range(0, rows, rc):
            for c0 in range(0, cols, stage_cols):
                chunks.append((src, dst, r0, rc, c0, min(stage_cols, cols - c0), scale))

    def copy(i):
        src, _, r0, rc, c0, cw, _ = chunks[i]
        slot = i % n_slots
        return pltpu.make_async_copy(
            src.at[pl.ds(r0, rc), pl.ds(c0, cw)],
            stage.at[slot, pl.ds(0, rc), pl.ds(0, cw)],
            sem.at[slot])

    ahead = n_slots - 1
    for i in range(min(ahead, len(chunks))):
        copy(i).start()
    for i in range(len(chunks)):
        if i + ahead < len(chunks):
            copy(i + ahead).start()
        copy(i).wait()
        _, dst, r0, rc, c0, cw, scale = chunks[i]
        w = stage[i % n_slots, pl.ds(0, rc), pl.ds(0, cw)]
        if scale is not None:
            w = w * scale[:, c0:c0 + cw]
        dst[pl.ds(r0, rc), pl.ds(c0, cw)] = w.astype(BF16)


def _pow2_div(x, n):
    assert n & (n - 1) == 0
    return x >> (n.bit_length() - 1)


def _pow2_mod(x, n):
    assert n & (n - 1) == 0
    return x & (n - 1)


def _block_diag_cols(compact, row_blk, col_blk, n_blocks):
    rows, ccols = compact.shape
    n_parts = ccols // col_blk
    part_w = n_blocks * col_blk
    ocols = n_parts * part_w
    k_i = lax.broadcasted_iota(jnp.int32, (ccols, ocols), 0)
    c_i = lax.broadcasted_iota(jnp.int32, (ccols, ocols), 1)
    onehot = ((_pow2_div(k_i, col_blk) == _pow2_div(c_i, part_w))
              & (_pow2_mod(k_i, col_blk) == _pow2_mod(c_i, col_blk)))
    tiled = _dot(compact.astype(BF16), onehot.astype(BF16))
    r_o = lax.broadcasted_iota(jnp.int32, (rows, ocols), 0)
    c_o = lax.broadcasted_iota(jnp.int32, (rows, ocols), 1)
    keep = _pow2_div(r_o, row_blk) == _pow2_div(_pow2_mod(c_o, part_w), col_blk)
    return jnp.where(keep, tiled, 0.0).astype(BF16)


def _block_diag_rows(compact, row_blk, col_blk, n_blocks):
    tiled = jnp.concatenate([compact] * n_blocks, axis=0)
    r_o = lax.broadcasted_iota(jnp.int32, tiled.shape, 0)
    c_o = lax.broadcasted_iota(jnp.int32, tiled.shape, 1)
    keep = _pow2_div(r_o, row_blk) == _pow2_div(c_o, col_blk)
    return jnp.where(keep, tiled, 0.0).astype(BF16)


def _mixer_kernel(vec, x_hbm, *refs):
    n_vec = len(vec.names)
    (win_hbm, wglu_hbm, waout_hbm, wbout_hbm, wo_hbm,
     bre_ref, bim_ref, cre_ref, cim_ref, wr_ref, wi_ref, o_hbm,
     win, wglu, waout, wbout, wo, stage, sem, bbd_ref, ccat_ref, wg_ref,
     xbuf, obuf, in_sem, out_sem,
     gates, ua, ubuf, xr, xi, s5st, abuf, bxbuf, lrust) = refs[n_vec:]
    _, tq, nb, d = xbuf.shape
    m = nb * tq
    s5w = ua.shape[1]
    nst = xr.shape[1]
    half_w = s5w // 2
    half_st = nst // 2
    lw = abuf.shape[1]
    hdr = (CONV_WIDTH - 1) * nb
    v = vec.view(refs[:n_vec])
    step = pl.program_id(0)
    n_steps = pl.num_programs(0)
    slot = step % 2

    def in_copy(tile, slot_, b):
        return pltpu.make_async_copy(
            x_hbm.at[b, pl.ds(tile * tq, tq), :], xbuf.at[slot_, :, b, :],
            in_sem.at[slot_, b])

    def out_copy(tile, slot_, b):
        return pltpu.make_async_copy(
            obuf.at[slot_, :, b, :], o_hbm.at[b, pl.ds(tile * tq, tq), :],
            out_sem.at[slot_, b])

    @pl.when(step == 0)
    def _():
        for b in range(nb):
            in_copy(0, 0, b).start()

    @pl.when(step + 1 < n_steps)
    def _():
        for b in range(nb):
            in_copy(step + 1, 1 - slot, b).start()

    @pl.when(step >= 2)
    def _():
        for b in range(nb):
            out_copy(step - 2, slot, b).wait()

    @pl.when(step == 0)
    def _():
        col = lax.broadcasted_iota(jnp.int32, (1, win.shape[1]), 1)
        in_scale = jnp.where(col >= s5w + lw, 0.5, 1.0)
        glu_scale = jnp.full((1, wglu.shape[1]), 0.5, F32)
        _stream_cast([(win_hbm.at[0], win, in_scale),
                      (wglu_hbm.at[0], wglu, glu_scale),
                      (waout_hbm.at[0], waout, None), (wbout_hbm.at[0], wbout, None),
                      (wo_hbm.at[0], wo, None)], stage, sem)
        s5_p = S5_GROUP_CH
        s5_n = cre_ref.shape[1]
        n_grp = bre_ref.shape[1] // s5_p
        for hh in range(bbd_ref.shape[0]):
            b_both = jnp.concatenate([bre_ref[hh], bim_ref[hh]], axis=1)
            bbd_ref[hh] = _block_diag_cols(b_both, s5_p, s5_n, n_grp)
            ccat_ref[hh, 0] = _block_diag_rows(cre_ref[hh], s5_n, s5_p, n_grp)
            ccat_ref[hh, 1] = _block_diag_rows(cim_ref[hh], s5_n, s5_p, n_grp)
        for j in range(wg_ref.shape[0]):
            w_both = jnp.concatenate([wr_ref[j], wi_ref[j]], axis=1)
            wg_ref[j] = _block_diag_cols(0.5 * w_both, LRU_HEAD_DIM, LRU_HEAD_DIM,
                                         MXU_DIM // LRU_HEAD_DIM)
        s5st[...] = jnp.zeros_like(s5st)
        lrust[...] = jnp.zeros_like(lrust)
        ubuf[pl.ds(0, hdr), :] = jnp.zeros((hdr, lw), F32)

    for b in range(nb):
        in_copy(step, slot, b).wait()
    xt = xbuf[slot].reshape(m, d)
    h = _rms(xt, v("g_mix")).astype(BF16)
    neg_sp = -LRU_C * _softplus(-v("lru_lambda"))
    g0 = s5w + lw
    n_blk = lw // MXU_DIM
    gate_w = 2 * d // n_blk

    def in_proj(lo, width, bias_scale=1.0):
        return _dot(h, win[:, lo:lo + width]) + bias_scale * v("b_in", lo, width)

    u = in_proj(0, s5w)
    ua[...] = u
    ub16 = u.astype(BF16)
    ub_all = in_proj(s5w, lw)
    ubuf[pl.ds(hdr, m), :] = ub_all

    def lru_block(j):
        cs = slice(j * MXU_DIM, (j + 1) * MXU_DIM)
        ub = ub_all[:, cs]
        xc = (v("conv_b", j * MXU_DIM, MXU_DIM)
              + v("conv_w", j * MXU_DIM, MXU_DIM, row=CONV_WIDTH - 1) * ub)
        for k in range(CONV_WIDTH - 1):
            xc = xc + (v("conv_w", j * MXU_DIM, MXU_DIM, row=k)
                       * ubuf[pl.ds(k * nb, m), cs])
        if j < bbd_ref.shape[0]:
            pr = _dot(ub16[:, j * half_w:(j + 1) * half_w], bbd_ref[j])
            xr[:, j * half_st:(j + 1) * half_st] = pr[:, :half_st]
            xi[:, j * half_st:(j + 1) * half_st] = pr[:, half_st:]
        g = _dot(xc.astype(BF16), wg_ref[j])
        r = _sigmoid_of_twice(g[:, :MXU_DIM] + 0.5 * v("b_r", j * MXU_DIM, MXU_DIM))
        ig = _sigmoid_of_twice(g[:, MXU_DIM:] + 0.5 * v("b_i", j * MXU_DIM, MXU_DIM))
        a = jnp.exp(r * neg_sp[:, cs])
        abuf[:, cs] = a
        bxbuf[:, cs] = _sqrt_nonneg(1.0 - a * a) * ig * xc
        gates[:, j * gate_w:(j + 1) * gate_w] = _sigmoid_of_twice(
            in_proj(g0 + j * gate_w, gate_w, bias_scale=0.5)).astype(BF16)
    n_pass = nst // SCAN_LANES
    lru_lanes = lw // n_pass

    def scan_pass(c):
        ls = slice(c * SCAN_LANES, (c + 1) * SCAN_LANES)
        ll = slice(c * lru_lanes, (c + 1) * lru_lanes)
        ar = jnp.broadcast_to(v("a_re", c * SCAN_LANES, SCAN_LANES), (nb, SCAN_LANES))
        ai = jnp.broadcast_to(v("a_im", c * SCAN_LANES, SCAN_LANES), (nb, SCAN_LANES))

        def scan_step(t, carry):
            sr, si, hl = carry
            rows = pl.ds(pl.multiple_of(t * nb, nb), nb)
            nsr = ar * sr - ai * si + xr[rows, ls]
            nsi = ar * si + ai * sr + xi[rows, ls]
            xr[rows, ls] = nsr
            xi[rows, ls] = nsi
            hn = abuf[rows, ll] * hl + bxbuf[rows, ll]
            bxbuf[rows, ll] = hn
            return nsr, nsi, hn

        sr, si, hl = lax.fori_loop(
            0, tq, scan_step, (s5st[0, :, ls], s5st[1, :, ls], lrust[:, ll]),
            unroll=tq)
        s5st[0, :, ls] = sr
        s5st[1, :, ls] = si
        lrust[:, ll] = hl

    def s5_readout(hh):
        ls = slice(hh * half_st, (hh + 1) * half_st)
        return (_dot(xr[:, ls].astype(BF16), ccat_ref[hh, 0])
                - _dot(xi[:, ls].astype(BF16), ccat_ref[hh, 1]))

    for j in range(n_blk):
        lru_block(j)
    ubuf[pl.ds(0, hdr), :] = ubuf[pl.ds(m, hdr), :]
    for c in range(n_pass):
        scan_pass(c)
    y0 = s5_readout(0)
    y1 = s5_readout(1)

    y = jnp.concatenate([y0, y1], axis=1) + v("s5_d") * ua[...]
    z = jax.nn.gelu(y)
    y_a = z * _sigmoid_of_twice(_dot(z.astype(BF16), wglu[...]) + 0.5 * v("b_glu"))
    merged = gates[:, 0:d] * _dot(y_a.astype(BF16), waout[...])
    merged = merged + gates[:, d:2 * d] * _dot(bxbuf[...].astype(BF16), wbout[...])
    x1 = xbuf[slot].reshape(m, d) + _dot(merged.astype(BF16), wo[...])
    obuf[slot] = x1.reshape(tq, nb, d)
    for b in range(nb):
        out_copy(step, slot, b).start()

    @pl.when(step == n_steps - 1)
    def _():
        for b in range(nb):
            out_copy(step, slot, b).wait()
        for b in range(nb):
            out_copy(step - 1, 1 - slot, b).wait()


def _const_spec(shape):
    nd = len(shape)
    return pl.BlockSpec(shape, lambda i, _nd=nd: (0,) * _nd,
                        pipeline_mode=pl.Buffered(1))


def _mixer_call(x, vec, dense, grouped):
    nb, seq, d = x.shape
    m = nb * TQ
    w_in, w_glu, w_a_out, w_b_out, w_o = dense
    s5w = w_glu.shape[1]
    bre, _, _, _, wr, _ = grouped
    n_half, half_w, s5_n = bre.shape
    half_st = half_w // S5_GROUP_CH * s5_n
    nst = n_half * half_st
    lw = w_b_out.shape[1]
    hbm_spec = pl.BlockSpec(memory_space=pl.ANY)
    in_specs = [hbm_spec] + [_const_spec(a.shape) for a in vec.arrays]
    in_specs += [pl.BlockSpec(memory_space=pl.ANY) for _ in dense]
    in_specs += [_const_spec(c.shape) for c in grouped]
    scratch = [pltpu.VMEM(w.shape[1:], BF16) for w in dense]
    scratch += [
        pltpu.VMEM((STAGE_SLOTS,) + MIXER_STAGE, F32),
        pltpu.SemaphoreType.DMA((STAGE_SLOTS,)),
        pltpu.VMEM((n_half, half_w, 2 * half_st), BF16),
        pltpu.VMEM((n_half, 2, half_st, half_w), BF16),
        pltpu.VMEM((wr.shape[0], MXU_DIM, 2 * MXU_DIM), BF16),
        pltpu.VMEM((2, TQ, nb, d), F32),
        pltpu.VMEM((2, TQ, nb, d), F32),
        pltpu.SemaphoreType.DMA((2, nb)),
        pltpu.SemaphoreType.DMA((2, nb)),
        pltpu.VMEM((m, 2 * d), BF16),
        pltpu.VMEM((m, s5w), F32),
        pltpu.VMEM((m + (CONV_WIDTH - 1) * nb, lw), F32),
        pltpu.VMEM((m, nst), F32),
        pltpu.VMEM((m, nst), F32),
        pltpu.VMEM((2, nb, nst), F32),
        pltpu.VMEM((m, lw), F32),
        pltpu.VMEM((m, lw), F32),
        pltpu.VMEM((nb, lw), F32),
    ]
    return pl.pallas_call(
        functools.partial(_mixer_kernel, vec),
        out_shape=jax.ShapeDtypeStruct(x.shape, F32),
        grid=(seq // TQ,),
        in_specs=in_specs,
        out_specs=hbm_spec,
        scratch_shapes=scratch,
        compiler_params=pltpu.CompilerParams(
            dimension_semantics=("arbitrary",), vmem_limit_bytes=VMEM_LIMIT),
        name="mixer",
    )(x, *vec.arrays, *dense, *grouped)


def _ffn_kernel(vec, x_ref, p_ref, *refs):
    n_vec = len(vec.names)
    (wgate_hbm, wup_hbm, wdown_hbm, wpg_hbm, wple_hbm, o_ref,
     wgate, wup, wdown, wpg, wple, stage, sem) = refs[n_vec:]
    v = vec.view(refs[:n_vec])

    @pl.when(pl.program_id(0) == 0)
    def _():
        _stream_cast([(wgate_hbm.at[0], wgate, None), (wup_hbm.at[0], wup, None),
                      (wdown_hbm.at[0], wdown, None), (wpg_hbm.at[0], wpg, None),
                      (wple_hbm.at[0], wple, None)], stage, sem)

    hid = wgate.shape[1]
    for s in range(FFN_ROWS // FFN_SUB_ROWS):
        rows = pl.ds(s * FFN_SUB_ROWS, FFN_SUB_ROWS)
        x = x_ref[rows, :]
        h2 = _rms(x, v("g_ffn")).astype(BF16)
        acc = x
        for j in range(hid // FFN_HID_BLK):
            cs = slice(j * FFN_HID_BLK, (j + 1) * FFN_HID_BLK)
            gt = _dot(h2, wgate[:, cs])
            a = gt * _sigmoid(gt) * _dot(h2, wup[:, cs])
            acc = acc + _dot(a.astype(BF16), wdown[cs, :])
        x2 = acc
        gate_p = _sigmoid(
            _dot(_rms(x2, v("g_ple_gate")).astype(BF16), wpg[...]) + v("b_ple_gate"))
        e = _rms(_dot(p_ref[rows, :].astype(BF16), wple[...]), v("g_ple"))
        x3 = x2 + gate_p * e
        o_ref[rows, :] = _rms(x3, v("g_final"))


def _ffn_call(x1, p, vec, dense):
    bsz, seq, d = x1.shape
    pd = p.shape[-1]
    per_seq = seq // FFN_ROWS
    in_specs = [
        pl.BlockSpec((None, FFN_ROWS, d), lambda i: (i // per_seq, i % per_seq, 0)),
        pl.BlockSpec((None, None, FFN_ROWS, pd),
                     lambda i: (0, i // per_seq, i % per_seq, 0))]
    in_specs += [_const_spec(a.shape) for a in vec.arrays]
    in_specs += [pl.BlockSpec(memory_space=pl.ANY) for _ in dense]
    scratch = [pltpu.VMEM(w.shape[1:], BF16) for w in dense]
    scratch += [pltpu.VMEM((STAGE_SLOTS,) + FFN_STAGE, F32),
                pltpu.SemaphoreType.DMA((STAGE_SLOTS,))]
    return pl.pallas_call(
        functools.partial(_ffn_kernel, vec),
        out_shape=jax.ShapeDtypeStruct((bsz, seq, d), F32),
        grid=(bsz * per_seq,),
        in_specs=in_specs,
        out_specs=pl.BlockSpec((None, FFN_ROWS, d),
                               lambda i: (i // per_seq, i % per_seq, 0)),
        scratch_shapes=scratch,
        compiler_params=pltpu.CompilerParams(
            dimension_semantics=("arbitrary",), vmem_limit_bytes=VMEM_LIMIT),
        name="ffn",
    )(x1, p, *vec.arrays, *dense)


def _s5_params(lam_re, lam_im, log_dt, b_re, b_im, c_re, c_im):
    g, n = lam_re.shape
    p = b_re.shape[2]
    hg = g // 2
    dt = jnp.exp(log_dt)[:, None]
    mag = jnp.exp(lam_re * dt)
    ar = mag * jnp.cos(lam_im * dt)
    ai = mag * jnp.sin(lam_im * dt)
    den = lam_re * lam_re + lam_im * lam_im
    nr = ar - 1.0
    fr = (nr * lam_re + ai * lam_im) / den
    fi = (ai * lam_re - nr * lam_im) / den
    bbr = fr[..., None] * b_re - fi[..., None] * b_im
    bbi = fr[..., None] * b_im + fi[..., None] * b_re
    def b_layout(b):
        return b.transpose(0, 2, 1).reshape(2, hg * p, n)

    def c_layout(c):
        return c.reshape(2, hg, p, n).transpose(0, 3, 1, 2).reshape(2, n, hg * p)

    return (ar.reshape(1, -1), ai.reshape(1, -1),
            b_layout(bbr), b_layout(bbi), c_layout(c_re), c_layout(c_im))


def _lru_gate_layout(w):
    heads, hd, _ = w.shape
    per = MXU_DIM // hd
    return w.reshape(heads // per, per * hd, hd)


def kernel(x, p, g_mix, w_in, b_in, lam_re, lam_im, log_dt, s5_b_re, s5_b_im, s5_c_re, s5_c_im, s5_d, w_glu, b_glu, conv_w, conv_b, w_r, b_r, w_i, b_i, lru_lambda, w_a_out, w_b_out, w_o, g_ffn, w_ffn_gate, w_ffn_up, w_ffn_down, g_ple_gate, w_ple_gate, b_ple_gate, w_ple, g_ple, g_final):
    bsz, seq, d = x.shape
    assert w_in.shape[0] == 1
    assert bsz == SUBLANES and seq % TQ == 0 and seq // TQ >= 2
    assert seq % FFN_ROWS == 0

    a_re, a_im, bre, bim, cre, cim = _s5_params(
        lam_re[0], lam_im[0], log_dt[0], s5_b_re[0], s5_b_im[0],
        s5_c_re[0], s5_c_im[0])

    def row(a):
        return a.reshape(1, -1)

    mixer_vec = _Vectors([
        ("g_mix", g_mix), ("b_in", b_in), ("a_re", a_re), ("a_im", a_im),
        ("s5_d", row(s5_d)), ("b_glu", b_glu), ("conv_w", conv_w[0]),
        ("conv_b", conv_b), ("b_r", row(b_r)), ("b_i", row(b_i)),
        ("lru_lambda", lru_lambda)])
    x1 = _mixer_call(x, mixer_vec, (w_in, w_glu, w_a_out, w_b_out, w_o),
                     (bre, bim, cre, cim,
                      _lru_gate_layout(w_r[0]), _lru_gate_layout(w_i[0])))
    ffn_vec = _Vectors([
        ("g_ffn", g_ffn), ("g_ple_gate", g_ple_gate), ("b_ple_gate", b_ple_gate),
        ("g_ple", g_ple), ("g_final", row(g_final))])
    return _ffn_call(x1, p, ffn_vec,
                     (w_ffn_gate, w_ffn_up, w_ffn_down, w_ple_gate, w_ple))
```

```python
import functools

import jax
import jax.numpy as jnp
from jax import lax
from jax.experimental import pallas as pl
from jax.experimental.pallas import tpu as pltpu

F32 = jnp.float32
BF16 = jnp.bfloat16

EPS = 1e-6
LRU_C = 8.0
CONV_WIDTH = 4
LRU_HEAD_DIM = 64
S5_GROUP_CH = 16

SUBLANES = 8
LANES = 128
MXU_DIM = 256
TQ = 64
SCAN_LANES = 512
STAGE_SLOTS = 6
MIXER_STAGE = (256, 1024)
FFN_ROWS = 1024
FFN_SUB_ROWS = 512
FFN_HID_BLK = 256
FFN_STAGE = (256, 1408)
VMEM_LIMIT = 56 * 1024 * 1024


def _rms(x, g):
    return x * lax.rsqrt(jnp.mean(x * x, axis=-1, keepdims=True) + EPS) * g


def _dot(a, b):
    return jnp.dot(a, b, preferred_element_type=F32)


def _sigmoid_of_twice(half_x):
    return 0.5 * jnp.tanh(half_x) + 0.5


def _sigmoid(x):
    return _sigmoid_of_twice(0.5 * x)


def _sqrt_nonneg(v):
    return jnp.where(v > 0.0, v * lax.rsqrt(v), 0.0)


def _softplus(x):
    return jnp.maximum(x, 0.0) + jnp.log1p(jnp.exp(-jnp.abs(x)))


class _Vectors:
    def __init__(self, named, flatten=()):
        self.names = [name for name, _ in named]
        self.arrays = [a.astype(F32) for _, a in named]
        self.flatten = [name for name in self.names if name in flatten]
        assert all(a.ndim == 2 for a in self.arrays)

    def flat_scratch(self):
        return [pltpu.VMEM((1, self.arrays[self.names.index(n)].size), F32)
                for n in self.flatten]

    def fill_flat(self, refs, flat_refs):
        for name, flat in zip(self.flatten, flat_refs):
            ref = refs[self.names.index(name)]
            flat[...] = jnp.concatenate(
                [ref[i:i + 1, :] for i in range(ref.shape[0])], axis=1)

    def view(self, refs, flat_refs=()):
        by_name = dict(zip(self.names, refs))
        by_name.update(zip(self.flatten, flat_refs))

        def get(name, lo=0, width=None, row=0):
            ref = by_name[name]
            width = ref.shape[1] - lo if width is None else width
            return ref[row:row + 1, lo:lo + width]
        return get


def _stream_cast(jobs, stage, sem):
    n_slots, stage_rows, stage_cols = stage.shape
    chunks = []
    for src, dst, scale in jobs:
        rows, cols = dst.shape
        rc = min(rows, stage_rows)
        assert rows % rc == 0 and cols % LANES == 0
        for r0 in range(0, rows, rc):
            for c0 in range(0, cols, stage_cols):
                chunks.append((src, dst, r0, rc, c0, min(stage_cols, cols - c0), scale))

    def copy(i):
        src, _, r0, rc, c0, cw, _ = chunks[i]
        slot = i % n_slots
        return pltpu.make_async_copy(
            src.at[pl.ds(r0, rc), pl.ds(c0, cw)],
            stage.at[slot, pl.ds(0, rc), pl.ds(0, cw)],
            sem.at[slot])

    ahead = n_slots - 1
    for i in range(min(ahead, len(chunks))):
        copy(i).start()
    for i in range(len(chunks)):
        if i + ahead < len(chunks):
            copy(i + ahead).start()
        copy(i).wait()
        _, dst, r0, rc, c0, cw, scale = chunks[i]
        w = stage[i % n_slots, pl.ds(0, rc), pl.ds(0, cw)]
        if scale is not None:
            w = w * scale[:, c0:c0 + cw]
        dst[pl.ds(r0, rc), pl.ds(c0, cw)] = w.astype(BF16)


def _pow2_div(x, n):
    assert n & (n - 1) == 0
    return x >> (n.bit_length() - 1)


def _pow2_mod(x, n):
    assert n & (n - 1) == 0
    return x & (n - 1)


def _block_diag_cols(compact, row_blk, col_blk, n_blocks):
    rows, ccols = compact.shape
    n_parts = ccols // col_blk
    part_w = n_blocks * col_blk
    ocols = n_parts * part_w
    k_i = lax.broadcasted_iota(jnp.int32, (ccols, ocols), 0)
    c_i = lax.broadcasted_iota(jnp.int32, (ccols, ocols), 1)
    onehot = ((_pow2_div(k_i, col_blk) == _pow2_div(c_i, part_w))
              & (_pow2_mod(k_i, col_blk) == _pow2_mod(c_i, col_blk)))
    tiled = _dot(compact.astype(BF16), onehot.astype(BF16))
    r_o = lax.broadcasted_iota(jnp.int32, (rows, ocols), 0)
    c_o = lax.broadcasted_iota(jnp.int32, (rows, ocols), 1)
    keep = _pow2_div(r_o, row_blk) == _pow2_div(_pow2_mod(c_o, part_w), col_blk)
    return jnp.where(keep, tiled, 0.0).astype(BF16)


def _block_diag_rows(compact, row_blk, col_blk, n_blocks):
    tiled = jnp.concatenate([compact] * n_blocks, axis=0)
    r_o = lax.broadcasted_iota(jnp.int32, tiled.shape, 0)
    c_o = lax.broadcasted_iota(jnp.int32, tiled.shape, 1)
    keep = _pow2_div(r_o, row_blk) == _pow2_div(c_o, col_blk)
    return jnp.where(keep, tiled, 0.0).astype(BF16)


def _mixer_kernel(vec, x_hbm, *refs):
    n_vec = len(vec.names)
    (win_hbm, wglu_hbm, waout_hbm, wbout_hbm, wo_hbm,
     bre_ref, bim_ref, cre_ref, cim_ref, wr_ref, wi_ref, o_hbm,
     win, wglu, waout, wbout, wo, stage, sem, bbd_ref, ccat_ref, wg_ref,
     xbuf, obuf, in_sem, out_sem,
     gates, ua, ubuf, xr, xi, s5st, abuf, bxbuf, lrust) = refs[n_vec:-len(vec.flatten)]
    flat_refs = refs[-len(vec.flatten):]
    _, tq, nb, d = xbuf.shape
    m = nb * tq
    s5w = ua.shape[1]
    nst = xr.shape[1]
    half_w = s5w // 2
    half_st = nst // 2
    lw = abuf.shape[1]
    hdr = (CONV_WIDTH - 1) * nb
    v = vec.view(refs[:n_vec], flat_refs)
    step = pl.program_id(0)
    n_steps = pl.num_programs(0)
    slot = step % 2

    def in_copy(tile, slot_, b):
        return pltpu.make_async_copy(
            x_hbm.at[b, pl.ds(tile * tq, tq), :], xbuf.at[slot_, :, b, :],
            in_sem.at[slot_, b])

    def out_copy(tile, slot_, b):
        return pltpu.make_async_copy(
            obuf.at[slot_, :, b, :], o_hbm.at[b, pl.ds(tile * tq, tq), :],
            out_sem.at[slot_, b])

    @pl.when(step == 0)
    def _():
        for b in range(nb):
            in_copy(0, 0, b).start()

    @pl.when(step + 1 < n_steps)
    def _():
        for b in range(nb):
            in_copy(step + 1, 1 - slot, b).start()

    @pl.when(step >= 2)
    def _():
        for b in range(nb):
            out_copy(step - 2, slot, b).wait()

    @pl.when(step == 0)
    def _():
        col = lax.broadcasted_iota(jnp.int32, (1, win.shape[1]), 1)
        in_scale = jnp.where(col >= s5w + lw, 0.5, 1.0)
        glu_scale = jnp.full((1, wglu.shape[1]), 0.5, F32)
        _stream_cast([(win_hbm.at[0], win, in_scale),
                      (wglu_hbm.at[0], wglu, glu_scale),
                      (waout_hbm.at[0], waout, None), (wbout_hbm.at[0], wbout, None),
                      (wo_hbm.at[0], wo, None)], stage, sem)
        s5_p = S5_GROUP_CH
        s5_n = cre_ref.shape[1]
        n_grp = bre_ref.shape[1] // s5_p
        for hh in range(bbd_ref.shape[0]):
            b_both = jnp.concatenate([bre_ref[hh], bim_ref[hh]], axis=1)
            bbd_ref[hh] = _block_diag_cols(b_both, s5_p, s5_n, n_grp)
            ccat_ref[hh, 0] = _block_diag_rows(cre_ref[hh], s5_n, s5_p, n_grp)
            ccat_ref[hh, 1] = _block_diag_rows(cim_ref[hh], s5_n, s5_p, n_grp)
        for j in range(wg_ref.shape[0]):
            w_both = jnp.concatenate([wr_ref[j], wi_ref[j]], axis=1)
            wg_ref[j] = _block_diag_cols(0.5 * w_both, LRU_HEAD_DIM, LRU_HEAD_DIM,
                                         MXU_DIM // LRU_HEAD_DIM)
        s5st[...] = jnp.zeros_like(s5st)
        lrust[...] = jnp.zeros_like(lrust)
        ubuf[pl.ds(0, hdr), :] = jnp.zeros((hdr, lw), F32)
        vec.fill_flat(refs[:n_vec], flat_refs)

    for b in range(nb):
        in_copy(step, slot, b).wait()
    xt = xbuf[slot].reshape(m, d)
    h = _rms(xt, v("g_mix")).astype(BF16)
    neg_sp = -LRU_C * _softplus(-v("lru_lambda"))
    g0 = s5w + lw
    n_blk = lw // MXU_DIM
    gate_w = 2 * d // n_blk

    def in_proj(lo, width, bias_scale=1.0):
        return _dot(h, win[:, lo:lo + width]) + bias_scale * v("b_in", lo, width)

    u = in_proj(0, s5w)
    ua[...] = u
    ub16 = u.astype(BF16)
    ub_all = in_proj(s5w, lw)
    ubuf[pl.ds(hdr, m), :] = ub_all

    def lru_block(j):
        cs = slice(j * MXU_DIM, (j + 1) * MXU_DIM)
        ub = ub_all[:, cs]
        xc = (v("conv_b", j * MXU_DIM, MXU_DIM)
              + v("conv_w", j * MXU_DIM, MXU_DIM, row=CONV_WIDTH - 1) * ub)
        for k in range(CONV_WIDTH - 1):
            xc = xc + (v("conv_w", j * MXU_DIM, MXU_DIM, row=k)
                       * ubuf[pl.ds(k * nb, m), cs])
        if j < bbd_ref.shape[0]:
            pr = _dot(ub16[:, j * half_w:(j + 1) * half_w], bbd_ref[j])
            xr[:, j * half_st:(j + 1) * half_st] = pr[:, :half_st]
            xi[:, j * half_st:(j + 1) * half_st] = pr[:, half_st:]
        g = _dot(xc.astype(BF16), wg_ref[j])
        r = _sigmoid_of_twice(g[:, :MXU_DIM] + 0.5 * v("b_r", j * MXU_DIM, MXU_DIM))
        ig = _sigmoid_of_twice(g[:, MXU_DIM:] + 0.5 * v("b_i", j * MXU_DIM, MXU_DIM))
        a = jnp.exp(r * neg_sp[:, cs])
        abuf[:, cs] = a
        bxbuf[:, cs] = _sqrt_nonneg(1.0 - a * a) * ig * xc
        gates[:, j * gate_w:(j + 1) * gate_w] = _sigmoid_of_twice(
            in_proj(g0 + j * gate_w, gate_w, bias_scale=0.5)).astype(BF16)
    n_pass = nst // SCAN_LANES
    lru_lanes = lw // n_pass

    def scan_pass(c):
        ls = slice(c * SCAN_LANES, (c + 1) * SCAN_LANES)
        ll = slice(c * lru_lanes, (c + 1) * lru_lanes)
        ar = jnp.broadcast_to(v("a_re", c * SCAN_LANES, SCAN_LANES), (nb, SCAN_LANES))
        ai = jnp.broadcast_to(v("a_im", c * SCAN_LANES, SCAN_LANES), (nb, SCAN_LANES))

        def scan_step(t, carry):
            sr, si, hl = carry
            rows = pl.ds(pl.multiple_of(t * nb, nb), nb)
            nsr = ar * sr - ai * si + xr[rows, ls]
            nsi = ar * si + ai * sr + xi[rows, ls]
            xr[rows, ls] = nsr
            xi[rows, ls] = nsi
            hn = abuf[rows, ll] * hl + bxbuf[rows, ll]
            bxbuf[rows, ll] = hn
            return nsr, nsi, hn

        sr, si, hl = lax.fori_loop(
            0, tq, scan_step, (s5st[0, :, ls], s5st[1, :, ls], lrust[:, ll]),
            unroll=tq)
        s5st[0, :, ls] = sr
        s5st[1, :, ls] = si
        lrust[:, ll] = hl

    def s5_readout(hh):
        ls = slice(hh * half_st, (hh + 1) * half_st)
        return (_dot(xr[:, ls].astype(BF16), ccat_ref[hh, 0])
                - _dot(xi[:, ls].astype(BF16), ccat_ref[hh, 1]))

    for j in range(n_blk):
        lru_block(j)
    ubuf[pl.ds(0, hdr), :] = ubuf[pl.ds(m, hdr), :]
    for c in range(n_pass):
        scan_pass(c)
    y0 = s5_readout(0)
    y1 = s5_readout(1)

    y = jnp.concatenate([y0, y1], axis=1) + v("s5_d") * ua[...]
    z = jax.nn.gelu(y)
    y_a = z * _sigmoid_of_twice(_dot(z.astype(BF16), wglu[...]) + 0.5 * v("b_glu"))
    merged = gates[:, 0:d] * _dot(y_a.astype(BF16), waout[...])
    merged = merged + gates[:, d:2 * d] * _dot(bxbuf[...].astype(BF16), wbout[...])
    x1 = xbuf[slot].reshape(m, d) + _dot(merged.astype(BF16), wo[...])
    obuf[slot] = x1.reshape(tq, nb, d)
    for b in range(nb):
        out_copy(step, slot, b).start()

    @pl.when(step == n_steps - 1)
    def _():
        for b in range(nb):
            out_copy(step, slot, b).wait()
        for b in range(nb):
            out_copy(step - 1, 1 - slot, b).wait()


def _const_spec(shape):
    nd = len(shape)
    return pl.BlockSpec(shape, lambda i, _nd=nd: (0,) * _nd,
                        pipeline_mode=pl.Buffered(1))


def _mixer_call(x, vec, dense, grouped):
    nb, seq, d = x.shape
    m = nb * TQ
    w_in, w_glu, w_a_out, w_b_out, w_o = dense
    s5w = w_glu.shape[1]
    bre, _, _, _, wr, _ = grouped
    n_half, half_w, s5_n = bre.shape
    half_st = half_w // S5_GROUP_CH * s5_n
    nst = n_half * half_st
    lw = w_b_out.shape[1]
    hbm_spec = pl.BlockSpec(memory_space=pl.ANY)
    in_specs = [hbm_spec] + [_const_spec(a.shape) for a in vec.arrays]
    in_specs += [pl.BlockSpec(memory_space=pl.ANY) for _ in dense]
    in_specs += [_const_spec(c.shape) for c in grouped]
    scratch = [pltpu.VMEM(w.shape[1:], BF16) for w in dense]
    scratch += [
        pltpu.VMEM((STAGE_SLOTS,) + MIXER_STAGE, F32),
        pltpu.SemaphoreType.DMA((STAGE_SLOTS,)),
        pltpu.VMEM((n_half, half_w, 2 * half_st), BF16),
        pltpu.VMEM((n_half, 2, half_st, half_w), BF16),
        pltpu.VMEM((wr.shape[0], MXU_DIM, 2 * MXU_DIM), BF16),
        pltpu.VMEM((2, TQ, nb, d), F32),
        pltpu.VMEM((2, TQ, nb, d), F32),
        pltpu.SemaphoreType.DMA((2, nb)),
        pltpu.SemaphoreType.DMA((2, nb)),
        pltpu.VMEM((m, 2 * d), BF16),
        pltpu.VMEM((m, s5w), F32),
        pltpu.VMEM((m + (CONV_WIDTH - 1) * nb, lw), F32),
        pltpu.VMEM((m, nst), F32),
        pltpu.VMEM((m, nst), F32),
        pltpu.VMEM((2, nb, nst), F32),
        pltpu.VMEM((m, lw), F32),
        pltpu.VMEM((m, lw), F32),
        pltpu.VMEM((nb, lw), F32),
    ] + vec.flat_scratch()
    return pl.pallas_call(
        functools.partial(_mixer_kernel, vec),
        out_shape=jax.ShapeDtypeStruct(x.shape, F32),
        grid=(seq // TQ,),
        in_specs=in_specs,
        out_specs=hbm_spec,
        scratch_shapes=scratch,
        compiler_params=pltpu.CompilerParams(
            dimension_semantics=("arbitrary",), vmem_limit_bytes=VMEM_LIMIT),
        name="mixer",
    )(x, *vec.arrays, *dense, *grouped)


def _ffn_kernel(vec, x_ref, p_ref, *refs):
    n_vec = len(vec.names)
    (wgate_hbm, wup_hbm, wdown_hbm, wpg_hbm, wple_hbm, o_ref,
     wgate, wup, wdown, wpg, wple, stage, sem) = refs[n_vec:]
    v = vec.view(refs[:n_vec])

    @pl.when(pl.program_id(0) == 0)
    def _():
        _stream_cast([(wgate_hbm.at[0], wgate, None), (wup_hbm.at[0], wup, None),
                      (wdown_hbm.at[0], wdown, None), (wpg_hbm.at[0], wpg, None),
                      (wple_hbm.at[0], wple, None)], stage, sem)

    hid = wgate.shape[1]
    for s in range(FFN_ROWS // FFN_SUB_ROWS):
        rows = pl.ds(s * FFN_SUB_ROWS, FFN_SUB_ROWS)
        x = x_ref[rows, :]
        h2 = _rms(x, v("g_ffn")).astype(BF16)
        acc = x
        for j in range(hid // FFN_HID_BLK):
            cs = slice(j * FFN_HID_BLK, (j + 1) * FFN_HID_BLK)
            gt = _dot(h2, wgate[:, cs])
            a = gt * _sigmoid(gt) * _dot(h2, wup[:, cs])
            acc = acc + _dot(a.astype(BF16), wdown[cs, :])
        x2 = acc
        gate_p = _sigmoid(
            _dot(_rms(x2, v("g_ple_gate")).astype(BF16), wpg[...]) + v("b_ple_gate"))
        e = _rms(_dot(p_ref[rows, :].astype(BF16), wple[...]), v("g_ple"))
        x3 = x2 + gate_p * e
        o_ref[rows, :] = _rms(x3, v("g_final"))


def _ffn_call(x1, p, vec, dense):
    bsz, seq, d = x1.shape
    pd = p.shape[-1]
    per_seq = seq // FFN_ROWS
    in_specs = [
        pl.BlockSpec((None, FFN_ROWS, d), lambda i: (i // per_seq, i % per_seq, 0)),
        pl.BlockSpec((None, None, FFN_ROWS, pd),
                     lambda i: (0, i // per_seq, i % per_seq, 0))]
    in_specs += [_const_spec(a.shape) for a in vec.arrays]
    in_specs += [pl.BlockSpec(memory_space=pl.ANY) for _ in dense]
    scratch = [pltpu.VMEM(w.shape[1:], BF16) for w in dense]
    scratch += [pltpu.VMEM((STAGE_SLOTS,) + FFN_STAGE, F32),
                pltpu.SemaphoreType.DMA((STAGE_SLOTS,))]
    return pl.pallas_call(
        functools.partial(_ffn_kernel, vec),
        out_shape=jax.ShapeDtypeStruct((bsz, seq, d), F32),
        grid=(bsz * per_seq,),
        in_specs=in_specs,
        out_specs=pl.BlockSpec((None, FFN_ROWS, d),
                               lambda i: (i // per_seq, i % per_seq, 0)),
        scratch_shapes=scratch,
        compiler_params=pltpu.CompilerParams(
            dimension_semantics=("arbitrary",), vmem_limit_bytes=VMEM_LIMIT),
        name="ffn",
    )(x1, p, *vec.arrays, *dense)


def _s5_params(lam_re, lam_im, log_dt, b_re, b_im, c_re, c_im):
    g, n = lam_re.shape
    p = b_re.shape[2]
    hg = g // 2
    dt = jnp.exp(log_dt)[:, None]
    mag = jnp.exp(lam_re * dt)
    ar = mag * jnp.cos(lam_im * dt)
    ai = mag * jnp.sin(lam_im * dt)
    den = lam_re * lam_re + lam_im * lam_im
    nr = ar - 1.0
    fr = (nr * lam_re + ai * lam_im) / den
    fi = (ai * lam_re - nr * lam_im) / den
    bbr = fr[..., None] * b_re - fi[..., None] * b_im
    bbi = fr[..., None] * b_im + fi[..., None] * b_re
    def b_layout(b):
        return b.transpose(0, 2, 1).reshape(2, hg * p, n)

    def c_layout(c):
        return c.reshape(2, hg, p, n).transpose(0, 3, 1, 2).reshape(2, n, hg * p)

    return (ar, ai,
            b_layout(bbr), b_layout(bbi), c_layout(c_re), c_layout(c_im))


def _lru_gate_layout(w):
    heads, hd, _ = w.shape
    per = MXU_DIM // hd
    return w.reshape(heads // per, per * hd, hd)


def kernel(x, p, g_mix, w_in, b_in, lam_re, lam_im, log_dt, s5_b_re, s5_b_im, s5_c_re, s5_c_im, s5_d, w_glu, b_glu, conv_w, conv_b, w_r, b_r, w_i, b_i, lru_lambda, w_a_out, w_b_out, w_o, g_ffn, w_ffn_gate, w_ffn_up, w_ffn_down, g_ple_gate, w_ple_gate, b_ple_gate, w_ple, g_ple, g_final):
    bsz, seq, d = x.shape
    assert w_in.shape[0] == 1
    assert bsz == SUBLANES and seq % TQ == 0 and seq // TQ >= 2
    assert seq % FFN_ROWS == 0

    a_re, a_im, bre, bim, cre, cim = _s5_params(
        lam_re[0], lam_im[0], log_dt[0], s5_b_re[0], s5_b_im[0],
        s5_c_re[0], s5_c_im[0])

    def row(a):
        return a.reshape(1, -1)

    mixer_vec = _Vectors([
        ("g_mix", g_mix), ("b_in", b_in), ("a_re", a_re), ("a_im", a_im),
        ("s5_d", s5_d[0]), ("b_glu", b_glu), ("conv_w", conv_w[0]),
        ("conv_b", conv_b), ("b_r", b_r[0]), ("b_i", b_i[0]),
        ("lru_lambda", lru_lambda)],
        flatten=("a_re", "a_im", "s5_d", "b_r", "b_i"))
    x1 = _mixer_call(x, mixer_vec, (w_in, w_glu, w_a_out, w_b_out, w_o),
                     (bre, bim, cre, cim,
                      _lru_gate_layout(w_r[0]), _lru_gate_layout(w_i[0])))
    ffn_vec = _Vectors([
        ("g_ffn", g_ffn), ("g_ple_gate", g_ple_gate), ("b_ple_gate", b_ple_gate),
        ("g_ple", g_ple), ("g_final", row(g_final))])
    return _ffn_call(x1, p, ffn_vec,
                     (w_ffn_gate, w_ffn_up, w_ffn_down, w_ple_gate, w_ple))
```

```python
import functools

import jax
import jax.numpy as jnp
from jax import lax
from jax.experimental import pallas as pl
from jax.experimental.pallas import tpu as pltpu

F32 = jnp.float32
BF16 = jnp.bfloat16

EPS = 1e-6
LRU_C = 8.0
CONV_WIDTH = 4
LRU_HEAD_DIM = 64
S5_GROUP_CH = 16

SUBLANES = 8
LANES = 128
MXU_DIM = 256
TQ = 64
SCAN_LANES = 512
STAGE_SLOTS = 6
MIXER_STAGE = (256, 1024)
FFN_ROWS = 1024
FFN_SUB_ROWS = 512
FFN_HID_BLK = 256
FFN_STAGE = (256, 1408)
VMEM_LIMIT = 56 * 1024 * 1024


def _rms(x, g):
    return x * lax.rsqrt(jnp.mean(x * x, axis=-1, keepdims=True) + EPS) * g


def _dot(a, b):
    return jnp.dot(a, b, preferred_element_type=F32)


def _sigmoid_of_twice(half_x):
    return 0.5 * jnp.tanh(half_x) + 0.5


def _sigmoid(x):
    return _sigmoid_of_twice(0.5 * x)


def _sqrt_nonneg(v):
    return jnp.where(v > 0.0, v * lax.rsqrt(v), 0.0)


def _softplus(x):
    return jnp.maximum(x, 0.0) + jnp.log1p(jnp.exp(-jnp.abs(x)))


class _Vectors:
    def __init__(self, named, flatten=()):
        self.names = [name for name, _ in named]
        self.arrays = [a.astype(F32) for _, a in named]
        self.flatten = [name for name in self.names if name in flatten]
        assert all(a.ndim == 2 for a in self.arrays)

    def flat_scratch(self):
        return [pltpu.VMEM((1, self.arrays[self.names.index(n)].size), F32)
                for n in self.flatten]

    def fill_flat(self, refs, flat_refs):
        for name, flat in zip(self.flatten, flat_refs):
            ref = refs[self.names.index(name)]
            flat[...] = jnp.concatenate(
                [ref[i:i + 1, :] for i in range(ref.shape[0])], axis=1)

    def view(self, refs, flat_refs=()):
        by_name = dict(zip(self.names, refs))
        by_name.update(zip(self.flatten, flat_refs))

        def get(name, lo=0, width=None, row=0):
            ref = by_name[name]
            width = ref.shape[1] - lo if width is None else width
            return ref[row:row + 1, lo:lo + width]
        return get


def _stream_cast(jobs, stage, sem):
    n_slots, stage_rows, stage_cols = stage.shape
    chunks = []
    for src, dst, scale in jobs:
        rows, cols = dst.shape
        rc = min(rows, stage_rows)
        assert rows % rc == 0 and cols % LANES == 0
        for r0 in range(0, rows, rc):
            for c0 in range(0, cols, stage_cols):
                chunks.append((src, dst, r0, rc, c0, min(stage_cols, cols - c0), scale))

    def copy(i):
        src, _, r0, rc, c0, cw, _ = chunks[i]
        slot = i % n_slots
        return pltpu.make_async_copy(
            src.at[pl.ds(r0, rc), pl.ds(c0, cw)],
            stage.at[slot, pl.ds(0, rc), pl.ds(0, cw)],
            sem.at[slot])

    ahead = n_slots - 1
    for i in range(min(ahead, len(chunks))):
        copy(i).start()
    for i in range(len(chunks)):
        if i + ahead < len(chunks):
            copy(i + ahead).start()
        copy(i).wait()
        _, dst, r0, rc, c0, cw, scale = chunks[i]
        w = stage[i % n_slots, pl.ds(0, rc), pl.ds(0, cw)]
        if scale is not None:
            w = w * scale[:, c0:c0 + cw]
        dst[pl.ds(r0, rc), pl.ds(c0, cw)] = w.astype(BF16)


def _pow2_div(x, n):
    assert n & (n - 1) == 0
    return x >> (n.bit_length() - 1)


def _pow2_mod(x, n):
    assert n & (n - 1) == 0
    return x & (n - 1)


def _block_diag_cols(compact, row_blk, col_blk, n_blocks):
    rows, ccols = compact.shape
    n_parts = ccols // col_blk
    part_w = n_blocks * col_blk
    ocols = n_parts * part_w
    k_i = lax.broadcasted_iota(jnp.int32, (ccols, ocols), 0)
    c_i = lax.broadcasted_iota(jnp.int32, (ccols, ocols), 1)
    onehot = ((_pow2_div(k_i, col_blk) == _pow2_div(c_i, part_w))
              & (_pow2_mod(k_i, col_blk) == _pow2_mod(c_i, col_blk)))
    tiled = _dot(compact.astype(BF16), onehot.astype(BF16))
    r_o = lax.broadcasted_iota(jnp.int32, (rows, ocols), 0)
    c_o = lax.broadcasted_iota(jnp.int32, (rows, ocols), 1)
    keep = _pow2_div(r_o, row_blk) == _pow2_div(_pow2_mod(c_o, part_w), col_blk)
    return jnp.where(keep, tiled, 0.0).astype(BF16)


def _block_diag_rows(compact, row_blk, col_blk, n_blocks):
    tiled = jnp.concatenate([compact] * n_blocks, axis=0)
    r_o = lax.broadcasted_iota(jnp.int32, tiled.shape, 0)
    c_o = lax.broadcasted_iota(jnp.int32, tiled.shape, 1)
    keep = _pow2_div(r_o, row_blk) == _pow2_div(c_o, col_blk)
    return jnp.where(keep, tiled, 0.0).astype(BF16)


def _mixer_kernel(vec, x_hbm, *refs):
    n_vec = len(vec.names)
    (win_hbm, wglu_hbm, waout_hbm, wbout_hbm, wo_hbm,
     bre_ref, bim_ref, cre_ref, cim_ref, wr_ref, wi_ref, o_hbm,
     win, wglu, waout, wbout, wo, stage, sem, bbd_ref, ccat_ref, wg_ref,
     xbuf, obuf, in_sem, out_sem,
     gates, ua, ubuf, xr, xi, s5st, abuf, bxbuf, lrust) = refs[n_vec:-len(vec.flatten)]
    flat_refs = refs[-len(vec.flatten):]
    _, tq, nb, d = xbuf.shape
    m = nb * tq
    s5w = ua.shape[1]
    nst = xr.shape[1]
    half_w = s5w // 2
    half_st = nst // 2
    lw = abuf.shape[1]
    hdr = (CONV_WIDTH - 1) * nb
    v = vec.view(refs[:n_vec], flat_refs)
    step = pl.program_id(0)
    n_steps = pl.num_programs(0)
    slot = step % 2

    def in_copy(tile, slot_, b):
        return pltpu.make_async_copy(
            x_hbm.at[b, pl.ds(tile * tq, tq), :], xbuf.at[slot_, :, b, :],
            in_sem.at[slot_, b])

    def out_copy(tile, slot_, b):
        return pltpu.make_async_copy(
            obuf.at[slot_, :, b, :], o_hbm.at[b, pl.ds(tile * tq, tq), :],
            out_sem.at[slot_, b])

    @pl.when(step == 0)
    def _():
        for b in range(nb):
            in_copy(0, 0, b).start()

    @pl.when(step + 1 < n_steps)
    def _():
        for b in range(nb):
            in_copy(step + 1, 1 - slot, b).start()

    @pl.when(step >= 2)
    def _():
        for b in range(nb):
            out_copy(step - 2, slot, b).wait()

    @pl.when(step == 0)
    def _():
        col = lax.broadcasted_iota(jnp.int32, (1, win.shape[1]), 1)
        in_scale = jnp.where(col >= s5w + lw, 0.5, 1.0)
        glu_scale = jnp.full((1, wglu.shape[1]), 0.5, F32)
        _stream_cast([(win_hbm.at[0], win, in_scale),
                      (wglu_hbm.at[0], wglu, glu_scale),
                      (waout_hbm.at[0], waout, None), (wbout_hbm.at[0], wbout, None),
                      (wo_hbm.at[0], wo, None)], stage, sem)
        s5_p = S5_GROUP_CH
        s5_n = cre_ref.shape[2]
        n_grp = bre_ref.shape[1] // s5_p
        for hh in range(bbd_ref.shape[0]):
            b_both = jnp.concatenate([bre_ref[hh], bim_ref[hh]], axis=1)
            bbd_ref[hh] = _block_diag_cols(b_both, s5_p, s5_n, n_grp)
            ccat_ref[hh, 0] = _block_diag_cols(cre_ref[hh], s5_p, s5_n, n_grp)
            ccat_ref[hh, 1] = _block_diag_cols(cim_ref[hh], s5_p, s5_n, n_grp)
        for j in range(wg_ref.shape[0]):
            w_both = jnp.concatenate([wr_ref[j], wi_ref[j]], axis=1)
            wg_ref[j] = _block_diag_cols(0.5 * w_both, LRU_HEAD_DIM, LRU_HEAD_DIM,
                                         MXU_DIM // LRU_HEAD_DIM)
        s5st[...] = jnp.zeros_like(s5st)
        lrust[...] = jnp.zeros_like(lrust)
        ubuf[pl.ds(0, hdr), :] = jnp.zeros((hdr, lw), F32)
        vec.fill_flat(refs[:n_vec], flat_refs)

    for b in range(nb):
        in_copy(step, slot, b).wait()
    xt = xbuf[slot].reshape(m, d)
    h = _rms(xt, v("g_mix")).astype(BF16)
    neg_sp = -LRU_C * _softplus(-v("lru_lambda"))
    g0 = s5w + lw
    n_blk = lw // MXU_DIM
    gate_w = 2 * d // n_blk

    def in_proj(lo, width, bias_scale=1.0):
        return _dot(h, win[:, lo:lo + width]) + bias_scale * v("b_in", lo, width)

    u = in_proj(0, s5w)
    ua[...] = u
    ub16 = u.astype(BF16)
    ub_all = in_proj(s5w, lw)
    ubuf[pl.ds(hdr, m), :] = ub_all

    def lru_block(j):
        cs = slice(j * MXU_DIM, (j + 1) * MXU_DIM)
        ub = ub_all[:, cs]
        xc = (v("conv_b", j * MXU_DIM, MXU_DIM)
              + v("conv_w", j * MXU_DIM, MXU_DIM, row=CONV_WIDTH - 1) * ub)
        for k in range(CONV_WIDTH - 1):
            xc = xc + (v("conv_w", j * MXU_DIM, MXU_DIM, row=k)
                       * ubuf[pl.ds(k * nb, m), cs])
        if j < bbd_ref.shape[0]:
            pr = _dot(ub16[:, j * half_w:(j + 1) * half_w], bbd_ref[j])
            xr[:, j * half_st:(j + 1) * half_st] = pr[:, :half_st]
            xi[:, j * half_st:(j + 1) * half_st] = pr[:, half_st:]
        g = _dot(xc.astype(BF16), wg_ref[j])
        r = _sigmoid_of_twice(g[:, :MXU_DIM] + 0.5 * v("b_r", j * MXU_DIM, MXU_DIM))
        ig = _sigmoid_of_twice(g[:, MXU_DIM:] + 0.5 * v("b_i", j * MXU_DIM, MXU_DIM))
        a = jnp.exp(r * neg_sp[:, cs])
        abuf[:, cs] = a
        bxbuf[:, cs] = _sqrt_nonneg(1.0 - a * a) * ig * xc
        gates[:, j * gate_w:(j + 1) * gate_w] = _sigmoid_of_twice(
            in_proj(g0 + j * gate_w, gate_w, bias_scale=0.5)).astype(BF16)
    n_pass = nst // SCAN_LANES
    lru_lanes = lw // n_pass

    def scan_pass(c):
        ls = slice(c * SCAN_LANES, (c + 1) * SCAN_LANES)
        ll = slice(c * lru_lanes, (c + 1) * lru_lanes)
        ar = jnp.broadcast_to(v("a_re", c * SCAN_LANES, SCAN_LANES), (nb, SCAN_LANES))
        ai = jnp.broadcast_to(v("a_im", c * SCAN_LANES, SCAN_LANES), (nb, SCAN_LANES))

        def scan_step(t, carry):
            sr, si, hl = carry
            rows = pl.ds(pl.multiple_of(t * nb, nb), nb)
            nsr = ar * sr - ai * si + xr[rows, ls]
            nsi = ar * si + ai * sr + xi[rows, ls]
            xr[rows, ls] = nsr
            xi[rows, ls] = nsi
            hn = abuf[rows, ll] * hl + bxbuf[rows, ll]
            bxbuf[rows, ll] = hn
            return nsr, nsi, hn

        sr, si, hl = lax.fori_loop(
            0, tq, scan_step, (s5st[0, :, ls], s5st[1, :, ls], lrust[:, ll]),
            unroll=tq)
        s5st[0, :, ls] = sr
        s5st[1, :, ls] = si
        lrust[:, ll] = hl

    def s5_readout(hh):
        ls = slice(hh * half_st, (hh + 1) * half_st)
        nt = (((1,), (1,)), ((), ()))
        return (lax.dot_general(xr[:, ls].astype(BF16), ccat_ref[hh, 0], nt,
                                preferred_element_type=F32)
                - lax.dot_general(xi[:, ls].astype(BF16), ccat_ref[hh, 1], nt,
                                  preferred_element_type=F32))

    for j in range(n_blk):
        lru_block(j)
    ubuf[pl.ds(0, hdr), :] = ubuf[pl.ds(m, hdr), :]
    for c in range(n_pass):
        scan_pass(c)
    y0 = s5_readout(0)
    y1 = s5_readout(1)

    y = jnp.concatenate([y0, y1], axis=1) + v("s5_d") * ua[...]
    z = jax.nn.gelu(y)
    y_a = z * _sigmoid_of_twice(_dot(z.astype(BF16), wglu[...]) + 0.5 * v("b_glu"))
    merged = gates[:, 0:d] * _dot(y_a.astype(BF16), waout[...])
    merged = merged + gates[:, d:2 * d] * _dot(bxbuf[...].astype(BF16), wbout[...])
    x1 = xbuf[slot].reshape(m, d) + _dot(merged.astype(BF16), wo[...])
    obuf[slot] = x1.reshape(tq, nb, d)
    for b in range(nb):
        out_copy(step, slot, b).start()

    @pl.when(step == n_steps - 1)
    def _():
        for b in range(nb):
            out_copy(step, slot, b).wait()
        for b in range(nb):
            out_copy(step - 1, 1 - slot, b).wait()


def _const_spec(shape):
    nd = len(shape)
    return pl.BlockSpec(shape, lambda i, _nd=nd: (0,) * _nd,
                        pipeline_mode=pl.Buffered(1))


def _mixer_call(x, vec, dense, grouped):
    nb, seq, d = x.shape
    m = nb * TQ
    w_in, w_glu, w_a_out, w_b_out, w_o = dense
    s5w = w_glu.shape[1]
    bre, _, _, _, wr, _ = grouped
    n_half, half_w, s5_n = bre.shape
    half_st = half_w // S5_GROUP_CH * s5_n
    nst = n_half * half_st
    lw = w_b_out.shape[1]
    hbm_spec = pl.BlockSpec(memory_space=pl.ANY)
    in_specs = [hbm_spec] + [_const_spec(a.shape) for a in vec.arrays]
    in_specs += [pl.BlockSpec(memory_space=pl.ANY) for _ in dense]
    in_specs += [_const_spec(c.shape) for c in grouped]
    scratch = [pltpu.VMEM(w.shape[1:], BF16) for w in dense]
    scratch += [
        pltpu.VMEM((STAGE_SLOTS,) + MIXER_STAGE, F32),
        pltpu.SemaphoreType.DMA((STAGE_SLOTS,)),
        pltpu.VMEM((n_half, half_w, 2 * half_st), BF16),
        pltpu.VMEM((n_half, 2, half_w, half_st), BF16),
        pltpu.VMEM((wr.shape[0], MXU_DIM, 2 * MXU_DIM), BF16),
        pltpu.VMEM((2, TQ, nb, d), F32),
        pltpu.VMEM((2, TQ, nb, d), F32),
        pltpu.SemaphoreType.DMA((2, nb)),
        pltpu.SemaphoreType.DMA((2, nb)),
        pltpu.VMEM((m, 2 * d), BF16),
        pltpu.VMEM((m, s5w), F32),
        pltpu.VMEM((m + (CONV_WIDTH - 1) * nb, lw), F32),
        pltpu.VMEM((m, nst), F32),
        pltpu.VMEM((m, nst), F32),
        pltpu.VMEM((2, nb, nst), F32),
        pltpu.VMEM((m, lw), F32),
        pltpu.VMEM((m, lw), F32),
        pltpu.VMEM((nb, lw), F32),
    ] + vec.flat_scratch()
    return pl.pallas_call(
        functools.partial(_mixer_kernel, vec),
        out_shape=jax.ShapeDtypeStruct(x.shape, F32),
        grid=(seq // TQ,),
        in_specs=in_specs,
        out_specs=hbm_spec,
        scratch_shapes=scratch,
        compiler_params=pltpu.CompilerParams(
            dimension_semantics=("arbitrary",), vmem_limit_bytes=VMEM_LIMIT),
        name="mixer",
    )(x, *vec.arrays, *dense, *grouped)


def _ffn_kernel(vec, x_ref, p_ref, *refs):
    n_vec = len(vec.names)
    (wgate_hbm, wup_hbm, wdown_hbm, wpg_hbm, wple_hbm, o_ref,
     wgate, wup, wdown, wpg, wple, stage, sem) = refs[n_vec:]
    v = vec.view(refs[:n_vec])

    @pl.when(pl.program_id(0) == 0)
    def _():
        _stream_cast([(wgate_hbm.at[0], wgate, None), (wup_hbm.at[0], wup, None),
                      (wdown_hbm.at[0], wdown, None), (wpg_hbm.at[0], wpg, None),
                      (wple_hbm.at[0], wple, None)], stage, sem)

    hid = wgate.shape[1]
    for s in range(FFN_ROWS // FFN_SUB_ROWS):
        rows = pl.ds(s * FFN_SUB_ROWS, FFN_SUB_ROWS)
        x = x_ref[rows, :]
        h2 = _rms(x, v("g_ffn")).astype(BF16)
        acc = x
        for j in range(hid // FFN_HID_BLK):
            cs = slice(j * FFN_HID_BLK, (j + 1) * FFN_HID_BLK)
            gt = _dot(h2, wgate[:, cs])
            a = gt * _sigmoid(gt) * _dot(h2, wup[:, cs])
            acc = acc + _dot(a.astype(BF16), wdown[cs, :])
        x2 = acc
        gate_p = _sigmoid(
            _dot(_rms(x2, v("g_ple_gate")).astype(BF16), wpg[...]) + v("b_ple_gate"))
        e = _rms(_dot(p_ref[rows, :].astype(BF16), wple[...]), v("g_ple"))
        x3 = x2 + gate_p * e
        o_ref[rows, :] = _rms(x3, v("g_final"))


def _ffn_call(x1, p, vec, dense):
    bsz, seq, d = x1.shape
    pd = p.shape[-1]
    per_seq = seq // FFN_ROWS
    in_specs = [
        pl.BlockSpec((None, FFN_ROWS, d), lambda i: (i // per_seq, i % per_seq, 0)),
        pl.BlockSpec((None, None, FFN_ROWS, pd),
                     lambda i: (0, i // per_seq, i % per_seq, 0))]
    in_specs += [_const_spec(a.shape) for a in vec.arrays]
    in_specs += [pl.BlockSpec(memory_space=pl.ANY) for _ in dense]
    scratch = [pltpu.VMEM(w.shape[1:], BF16) for w in dense]
    scratch += [pltpu.VMEM((STAGE_SLOTS,) + FFN_STAGE, F32),
                pltpu.SemaphoreType.DMA((STAGE_SLOTS,))]
    return pl.pallas_call(
        functools.partial(_ffn_kernel, vec),
        out_shape=jax.ShapeDtypeStruct((bsz, seq, d), F32),
        grid=(bsz * per_seq,),
        in_specs=in_specs,
        out_specs=pl.BlockSpec((None, FFN_ROWS, d),
                               lambda i: (i // per_seq, i % per_seq, 0)),
        scratch_shapes=scratch,
        compiler_params=pltpu.CompilerParams(
            dimension_semantics=("arbitrary",), vmem_limit_bytes=VMEM_LIMIT),
        name="ffn",
    )(x1, p, *vec.arrays, *dense)


def _s5_params(lam_re, lam_im, log_dt, b_re, b_im, c_re, c_im):
    g, n = lam_re.shape
    p = b_re.shape[2]
    hg = g // 2
    dt = jnp.exp(log_dt)[:, None]
    mag = jnp.exp(lam_re * dt)
    ar = mag * jnp.cos(lam_im * dt)
    ai = mag * jnp.sin(lam_im * dt)
    den = lam_re * lam_re + lam_im * lam_im
    nr = ar - 1.0
    fr = (nr * lam_re + ai * lam_im) / den
    fi = (ai * lam_re - nr * lam_im) / den
    bbr = fr[..., None] * b_re - fi[..., None] * b_im
    bbi = fr[..., None] * b_im + fi[..., None] * b_re
    def b_layout(b):
        return b.transpose(0, 2, 1).reshape(2, hg * p, n)

    def c_layout(c):
        return c.reshape(2, hg * p, n)

    return (ar, ai,
            b_layout(bbr), b_layout(bbi), c_layout(c_re), c_layout(c_im))


def _lru_gate_layout(w):
    heads, hd, _ = w.shape
    per = MXU_DIM // hd
    return w.reshape(heads // per, per * hd, hd)


def kernel(x, p, g_mix, w_in, b_in, lam_re, lam_im, log_dt, s5_b_re, s5_b_im, s5_c_re, s5_c_im, s5_d, w_glu, b_glu, conv_w, conv_b, w_r, b_r, w_i, b_i, lru_lambda, w_a_out, w_b_out, w_o, g_ffn, w_ffn_gate, w_ffn_up, w_ffn_down, g_ple_gate, w_ple_gate, b_ple_gate, w_ple, g_ple, g_final):
    bsz, seq, d = x.shape
    assert w_in.shape[0] == 1
    assert bsz == SUBLANES and seq % TQ == 0 and seq // TQ >= 2
    assert seq % FFN_ROWS == 0

    a_re, a_im, bre, bim, cre, cim = _s5_params(
        lam_re[0], lam_im[0], log_dt[0], s5_b_re[0], s5_b_im[0],
        s5_c_re[0], s5_c_im[0])

    def row(a):
        return a.reshape(1, -1)

    mixer_vec = _Vectors([
        ("g_mix", g_mix), ("b_in", b_in), ("a_re", a_re), ("a_im", a_im),
        ("s5_d", s5_d[0]), ("b_glu", b_glu), ("conv_w", conv_w[0]),
        ("conv_b", conv_b), ("b_r", b_r[0]), ("b_i", b_i[0]),
        ("lru_lambda", lru_lambda)],
        flatten=("a_re", "a_im", "s5_d", "b_r", "b_i"))
    x1 = _mixer_call(x, mixer_vec, (w_in, w_glu, w_a_out, w_b_out, w_o),
                     (bre, bim, cre, cim,
                      _lru_gate_layout(w_r[0]), _lru_gate_layout(w_i[0])))
    ffn_vec = _Vectors([
        ("g_ffn", g_ffn), ("g_ple_gate", g_ple_gate), ("b_ple_gate", b_ple_gate),
        ("g_ple", g_ple), ("g_final", row(g_final))])
    return _ffn_call(x1, p, ffn_vec,
                     (w_ffn_gate, w_ffn_up, w_ffn_down, w_ple_gate, w_ple))
```

```python
import functools

import jax
import jax.numpy as jnp
from jax import lax
from jax.experimental import pallas as pl
from jax.experimental.pallas import tpu as pltpu

F32 = jnp.float32
BF16 = jnp.bfloat16

EPS = 1e-6
LRU_C = 8.0
CONV_WIDTH = 4
LRU_HEAD_DIM = 64
S5_GROUP_CH = 16

SUBLANES = 8
LANES = 128
MXU_DIM = 256
TQ = 64
SCAN_LANES = 512
STAGE_SLOTS = 6
MIXER_STAGE = (256, 1024)
FFN_ROWS = 1024
FFN_SUB_ROWS = 512
FFN_HID_BLK = 256
FFN_STAGE = (256, 1408)
VMEM_LIMIT = 56 * 1024 * 1024


def _rms(x, g):
    return x * lax.rsqrt(jnp.mean(x * x, axis=-1, keepdims=True) + EPS) * g


def _dot(a, b):
    return jnp.dot(a, b, preferred_element_type=F32)


def _sigmoid_of_twice(half_x):
    return 0.5 * jnp.tanh(half_x) + 0.5


def _sigmoid(x):
    return _sigmoid_of_twice(0.5 * x)


def _sqrt_nonneg(v):
    return jnp.where(v > 0.0, v * lax.rsqrt(v), 0.0)


def _softplus(x):
    return jnp.maximum(x, 0.0) + jnp.log1p(jnp.exp(-jnp.abs(x)))


class _Vectors:
    def __init__(self, named, flatten=()):
        self.names = [name for name, _ in named]
        self.arrays = [a.astype(F32) for _, a in named]
        self.flatten = [name for name in self.names if name in flatten]
        assert all(a.ndim == 2 for a in self.arrays)

    def flat_scratch(self):
        return [pltpu.VMEM((1, self.arrays[self.names.index(n)].size), F32)
                for n in self.flatten]

    def fill_flat(self, refs, flat_refs):
        for name, flat in zip(self.flatten, flat_refs):
            ref = refs[self.names.index(name)]
            flat[...] = jnp.concatenate(
                [ref[i:i + 1, :] for i in range(ref.shape[0])], axis=1)

    def view(self, refs, flat_refs=()):
        by_name = dict(zip(self.names, refs))
        by_name.update(zip(self.flatten, flat_refs))

        def get(name, lo=0, width=None, row=0):
            ref = by_name[name]
            width = ref.shape[1] - lo if width is None else width
            return ref[row:row + 1, lo:lo + width]
        return get


def _stream_cast(jobs, stage, sem):
    n_slots, stage_rows, stage_cols = stage.shape
    chunks = []
    for src, dst, scale in jobs:
        rows, cols = dst.shape
        rc = min(rows, stage_rows)
        assert rows % rc == 0 and cols % LANES == 0
        for r0 in range(0, rows, rc):
            for c0 in range(0, cols, stage_cols):
                chunks.append((src, dst, r0, rc, c0, min(stage_cols, cols - c0), scale))

    def copy(i):
        src, _, r0, rc, c0, cw, _ = chunks[i]
        slot = i % n_slots
        return pltpu.make_async_copy(
            src.at[pl.ds(r0, rc), pl.ds(c0, cw)],
            stage.at[slot, pl.ds(0, rc), pl.ds(0, cw)],
            sem.at[slot])

    ahead = n_slots - 1
    for i in range(min(ahead, len(chunks))):
        copy(i).start()
    for i in range(len(chunks)):
        if i + ahead < len(chunks):
            copy(i + ahead).start()
        copy(i).wait()
        _, dst, r0, rc, c0, cw, scale = chunks[i]
        w = stage[i % n_slots, pl.ds(0, rc), pl.ds(0, cw)]
        if scale is not None:
            w = w * scale[:, c0:c0 + cw]
        dst[pl.ds(r0, rc), pl.ds(c0, cw)] = w.astype(BF16)


def _pow2_div(x, n):
    assert n & (n - 1) == 0
    return x >> (n.bit_length() - 1)


def _pow2_mod(x, n):
    assert n & (n - 1) == 0
    return x & (n - 1)


def _block_diag_cols(compact, row_blk, col_blk, n_blocks):
    rows, ccols = compact.shape
    n_parts = ccols // col_blk
    part_w = n_blocks * col_blk
    ocols = n_parts * part_w
    k_i = lax.broadcasted_iota(jnp.int32, (ccols, ocols), 0)
    c_i = lax.broadcasted_iota(jnp.int32, (ccols, ocols), 1)
    onehot = ((_pow2_div(k_i, col_blk) == _pow2_div(c_i, part_w))
              & (_pow2_mod(k_i, col_blk) == _pow2_mod(c_i, col_blk)))
    tiled = _dot(compact.astype(BF16), onehot.astype(BF16))
    r_o = lax.broadcasted_iota(jnp.int32, (rows, ocols), 0)
    c_o = lax.broadcasted_iota(jnp.int32, (rows, ocols), 1)
    keep = _pow2_div(r_o, row_blk) == _pow2_div(_pow2_mod(c_o, part_w), col_blk)
    return jnp.where(keep, tiled, 0.0).astype(BF16)


def _mixer_kernel(vec, x_hbm, *refs):
    n_vec = len(vec.names)
    (win_hbm, wglu_hbm, waout_hbm, wbout_hbm, wo_hbm,
     bre_ref, bim_ref, cre_ref, cim_ref, wr_ref, wi_ref, o_hbm,
     win, wglu, waout, wbout, wo, stage, sem, bbd_ref, ccat_ref, wg_ref,
     xbuf, obuf, in_sem, out_sem,
     gates, ua, ubuf, xr, xi, s5st, abuf, bxbuf, lrust) = refs[n_vec:-len(vec.flatten)]
    flat_refs = refs[-len(vec.flatten):]
    _, tq, nb, d = xbuf.shape
    m = nb * tq
    s5w = ua.shape[1]
    nst = xr.shape[1]
    half_w = s5w // 2
    half_st = nst // 2
    lw = abuf.shape[1]
    hdr = (CONV_WIDTH - 1) * nb
    v = vec.view(refs[:n_vec], flat_refs)
    step = pl.program_id(0)
    n_steps = pl.num_programs(0)
    slot = step % 2

    def in_copy(tile, slot_, b):
        return pltpu.make_async_copy(
            x_hbm.at[b, pl.ds(tile * tq, tq), :], xbuf.at[slot_, :, b, :],
            in_sem.at[slot_, b])

    def out_copy(tile, slot_, b):
        return pltpu.make_async_copy(
            obuf.at[slot_, :, b, :], o_hbm.at[b, pl.ds(tile * tq, tq), :],
            out_sem.at[slot_, b])

    @pl.when(step == 0)
    def _():
        for b in range(nb):
            in_copy(0, 0, b).start()

    @pl.when(step + 1 < n_steps)
    def _():
        for b in range(nb):
            in_copy(step + 1, 1 - slot, b).start()

    @pl.when(step >= 2)
    def _():
        for b in range(nb):
            out_copy(step - 2, slot, b).wait()

    @pl.when(step == 0)
    def _():
        col = lax.broadcasted_iota(jnp.int32, (1, win.shape[1]), 1)
        in_scale = jnp.where(col >= s5w + lw, 0.5, 1.0)
        glu_scale = jnp.full((1, wglu.shape[1]), 0.5, F32)
        _stream_cast([(win_hbm.at[0], win, in_scale),
                      (wglu_hbm.at[0], wglu, glu_scale),
                      (waout_hbm.at[0], waout, None), (wbout_hbm.at[0], wbout, None),
                      (wo_hbm.at[0], wo, None)], stage, sem)
        s5_p = S5_GROUP_CH
        s5_n = cre_ref.shape[2]
        n_grp = bre_ref.shape[1] // s5_p
        for hh in range(bbd_ref.shape[0]):
            b_both = jnp.concatenate([bre_ref[hh], bim_ref[hh]], axis=1)
            bbd_ref[hh] = _block_diag_cols(b_both, s5_p, s5_n, n_grp)
            ccat_ref[hh, 0] = _block_diag_cols(cre_ref[hh], s5_p, s5_n, n_grp)
            ccat_ref[hh, 1] = _block_diag_cols(cim_ref[hh], s5_p, s5_n, n_grp)
        for j in range(wg_ref.shape[0]):
            w_both = jnp.concatenate([wr_ref[j], wi_ref[j]], axis=1)
            wg_ref[j] = _block_diag_cols(0.5 * w_both, LRU_HEAD_DIM, LRU_HEAD_DIM,
                                         MXU_DIM // LRU_HEAD_DIM)
        s5st[...] = jnp.zeros_like(s5st)
        lrust[...] = jnp.zeros_like(lrust)
        ubuf[pl.ds(0, hdr), :] = jnp.zeros((hdr, lw), F32)
        vec.fill_flat(refs[:n_vec], flat_refs)

    for b in range(nb):
        in_copy(step, slot, b).wait()
    xt = xbuf[slot].reshape(m, d)
    h = _rms(xt, v("g_mix")).astype(BF16)
    neg_sp = -LRU_C * _softplus(-v("lru_lambda"))
    g0 = s5w + lw
    n_blk = lw // MXU_DIM
    gate_w = 2 * d // n_blk

    def in_proj(lo, width, bias_scale=1.0):
        return _dot(h, win[:, lo:lo + width]) + bias_scale * v("b_in", lo, width)

    u = in_proj(0, s5w)
    ua[...] = u
    ub16 = u.astype(BF16)
    ub_all = in_proj(s5w, lw)
    ubuf[pl.ds(hdr, m), :] = ub_all

    def lru_block(j):
        cs = slice(j * MXU_DIM, (j + 1) * MXU_DIM)
        ub = ub_all[:, cs]
        xc = (v("conv_b", j * MXU_DIM, MXU_DIM)
              + v("conv_w", j * MXU_DIM, MXU_DIM, row=CONV_WIDTH - 1) * ub)
        for k in range(CONV_WIDTH - 1):
            xc = xc + (v("conv_w", j * MXU_DIM, MXU_DIM, row=k)
                       * ubuf[pl.ds(k * nb, m), cs])
        if j < bbd_ref.shape[0]:
            pr = _dot(ub16[:, j * half_w:(j + 1) * half_w], bbd_ref[j])
            xr[:, j * half_st:(j + 1) * half_st] = pr[:, :half_st]
            xi[:, j * half_st:(j + 1) * half_st] = pr[:, half_st:]
        g = _dot(xc.astype(BF16), wg_ref[j])
        r = _sigmoid_of_twice(g[:, :MXU_DIM] + 0.5 * v("b_r", j * MXU_DIM, MXU_DIM))
        ig = _sigmoid_of_twice(g[:, MXU_DIM:] + 0.5 * v("b_i", j * MXU_DIM, MXU_DIM))
        a = jnp.exp(r * neg_sp[:, cs])
        abuf[:, cs] = a
        bxbuf[:, cs] = _sqrt_nonneg(1.0 - a * a) * ig * xc
        gates[:, j * gate_w:(j + 1) * gate_w] = _sigmoid_of_twice(
            in_proj(g0 + j * gate_w, gate_w, bias_scale=0.5)).astype(BF16)
    n_pass = nst // SCAN_LANES
    lru_lanes = lw // n_pass

    def scan_pass(c):
        ls = slice(c * SCAN_LANES, (c + 1) * SCAN_LANES)
        ll = slice(c * lru_lanes, (c + 1) * lru_lanes)
        ar = jnp.broadcast_to(v("a_re", c * SCAN_LANES, SCAN_LANES), (nb, SCAN_LANES))
        ai = jnp.broadcast_to(v("a_im", c * SCAN_LANES, SCAN_LANES), (nb, SCAN_LANES))

        def scan_step(t, carry):
            sr, si, hl = carry
            rows = pl.ds(pl.multiple_of(t * nb, nb), nb)
            nsr = ar * sr - ai * si + xr[rows, ls]
            nsi = ar * si + ai * sr + xi[rows, ls]
            xr[rows, ls] = nsr
            xi[rows, ls] = nsi
            hn = abuf[rows, ll] * hl + bxbuf[rows, ll]
            bxbuf[rows, ll] = hn
            return nsr, nsi, hn

        sr, si, hl = lax.fori_loop(
            0, tq, scan_step, (s5st[0, :, ls], s5st[1, :, ls], lrust[:, ll]),
            unroll=tq)
        s5st[0, :, ls] = sr
        s5st[1, :, ls] = si
        lrust[:, ll] = hl

    def s5_readout(hh):
        ls = slice(hh * half_st, (hh + 1) * half_st)
        nt = (((1,), (1,)), ((), ()))
        return (lax.dot_general(xr[:, ls].astype(BF16), ccat_ref[hh, 0], nt,
                                preferred_element_type=F32)
                - lax.dot_general(xi[:, ls].astype(BF16), ccat_ref[hh, 1], nt,
                                  preferred_element_type=F32))

    for j in range(n_blk):
        lru_block(j)
    ubuf[pl.ds(0, hdr), :] = ubuf[pl.ds(m, hdr), :]
    for c in range(n_pass):
        scan_pass(c)
    y0 = s5_readout(0)
    y1 = s5_readout(1)

    y = jnp.concatenate([y0, y1], axis=1) + v("s5_d") * ua[...]
    z = jax.nn.gelu(y)
    y_a = z * _sigmoid_of_twice(_dot(z.astype(BF16), wglu[...]) + 0.5 * v("b_glu"))
    merged = gates[:, 0:d] * _dot(y_a.astype(BF16), waout[...])
    merged = merged + gates[:, d:2 * d] * _dot(bxbuf[...].astype(BF16), wbout[...])
    x1 = xbuf[slot].reshape(m, d) + _dot(merged.astype(BF16), wo[...])
    obuf[slot] = x1.reshape(tq, nb, d)
    for b in range(nb):
        out_copy(step, slot, b).start()

    @pl.when(step == n_steps - 1)
    def _():
        for b in range(nb):
            out_copy(step, slot, b).wait()
        for b in range(nb):
            out_copy(step - 1, 1 - slot, b).wait()


def _const_spec(shape):
    nd = len(shape)
    return pl.BlockSpec(shape, lambda i, _nd=nd: (0,) * _nd,
                        pipeline_mode=pl.Buffered(1))


def _mixer_call(x, vec, dense, grouped):
    nb, seq, d = x.shape
    m = nb * TQ
    w_in, w_glu, w_a_out, w_b_out, w_o = dense
    s5w = w_glu.shape[1]
    bre, _, _, _, wr, _ = grouped
    n_half, half_w, s5_n = bre.shape
    half_st = half_w // S5_GROUP_CH * s5_n
    nst = n_half * half_st
    lw = w_b_out.shape[1]
    hbm_spec = pl.BlockSpec(memory_space=pl.ANY)
    in_specs = [hbm_spec] + [_const_spec(a.shape) for a in vec.arrays]
    in_specs += [pl.BlockSpec(memory_space=pl.ANY) for _ in dense]
    in_specs += [_const_spec(c.shape) for c in grouped]
    scratch = [pltpu.VMEM(w.shape[1:], BF16) for w in dense]
    scratch += [
        pltpu.VMEM((STAGE_SLOTS,) + MIXER_STAGE, F32),
        pltpu.SemaphoreType.DMA((STAGE_SLOTS,)),
        pltpu.VMEM((n_half, half_w, 2 * half_st), BF16),
        pltpu.VMEM((n_half, 2, half_w, half_st), BF16),
        pltpu.VMEM((wr.shape[0], MXU_DIM, 2 * MXU_DIM), BF16),
        pltpu.VMEM((2, TQ, nb, d), F32),
        pltpu.VMEM((2, TQ, nb, d), F32),
        pltpu.SemaphoreType.DMA((2, nb)),
        pltpu.SemaphoreType.DMA((2, nb)),
        pltpu.VMEM((m, 2 * d), BF16),
        pltpu.VMEM((m, s5w), F32),
        pltpu.VMEM((m + (CONV_WIDTH - 1) * nb, lw), F32),
        pltpu.VMEM((m, nst), F32),
        pltpu.VMEM((m, nst), F32),
        pltpu.VMEM((2, nb, nst), F32),
        pltpu.VMEM((m, lw), F32),
        pltpu.VMEM((m, lw), F32),
        pltpu.VMEM((nb, lw), F32),
    ] + vec.flat_scratch()
    return pl.pallas_call(
        functools.partial(_mixer_kernel, vec),
        out_shape=jax.ShapeDtypeStruct(x.shape, F32),
        grid=(seq // TQ,),
        in_specs=in_specs,
        out_specs=hbm_spec,
        scratch_shapes=scratch,
        compiler_params=pltpu.CompilerParams(
            dimension_semantics=("arbitrary",), vmem_limit_bytes=VMEM_LIMIT),
        name="mixer",
    )(x, *vec.arrays, *dense, *grouped)


def _ffn_kernel(vec, x_ref, p_ref, *refs):
    n_vec = len(vec.names)
    (wgate_hbm, wup_hbm, wdown_hbm, wpg_hbm, wple_hbm, o_ref,
     wgate, wup, wdown, wpg, wple, stage, sem) = refs[n_vec:]
    v = vec.view(refs[:n_vec])

    @pl.when(pl.program_id(0) == 0)
    def _():
        _stream_cast([(wgate_hbm.at[0], wgate, None), (wup_hbm.at[0], wup, None),
                      (wdown_hbm.at[0], wdown, None), (wpg_hbm.at[0], wpg, None),
                      (wple_hbm.at[0], wple, None)], stage, sem)

    hid = wgate.shape[1]
    for s in range(FFN_ROWS // FFN_SUB_ROWS):
        rows = pl.ds(s * FFN_SUB_ROWS, FFN_SUB_ROWS)
        x = x_ref[rows, :]
        h2 = _rms(x, v("g_ffn")).astype(BF16)
        acc = x
        for j in range(hid // FFN_HID_BLK):
            cs = slice(j * FFN_HID_BLK, (j + 1) * FFN_HID_BLK)
            gt = _dot(h2, wgate[:, cs])
            a = gt * _sigmoid(gt) * _dot(h2, wup[:, cs])
            acc = acc + _dot(a.astype(BF16), wdown[cs, :])
        x2 = acc
        gate_p = _sigmoid(
            _dot(_rms(x2, v("g_ple_gate")).astype(BF16), wpg[...]) + v("b_ple_gate"))
        e = _rms(_dot(p_ref[rows, :].astype(BF16), wple[...]), v("g_ple"))
        x3 = x2 + gate_p * e
        o_ref[rows, :] = _rms(x3, v("g_final"))


def _ffn_call(x1, p, vec, dense):
    bsz, seq, d = x1.shape
    pd = p.shape[-1]
    per_seq = seq // FFN_ROWS
    in_specs = [
        pl.BlockSpec((None, FFN_ROWS, d), lambda i: (i // per_seq, i % per_seq, 0)),
        pl.BlockSpec((None, None, FFN_ROWS, pd),
                     lambda i: (0, i // per_seq, i % per_seq, 0))]
    in_specs += [_const_spec(a.shape) for a in vec.arrays]
    in_specs += [pl.BlockSpec(memory_space=pl.ANY) for _ in dense]
    scratch = [pltpu.VMEM(w.shape[1:], BF16) for w in dense]
    scratch += [pltpu.VMEM((STAGE_SLOTS,) + FFN_STAGE, F32),
                pltpu.SemaphoreType.DMA((STAGE_SLOTS,))]
    return pl.pallas_call(
        functools.partial(_ffn_kernel, vec),
        out_shape=jax.ShapeDtypeStruct((bsz, seq, d), F32),
        grid=(bsz * per_seq,),
        in_specs=in_specs,
        out_specs=pl.BlockSpec((None, FFN_ROWS, d),
                               lambda i: (i // per_seq, i % per_seq, 0)),
        scratch_shapes=scratch,
        compiler_params=pltpu.CompilerParams(
            dimension_semantics=("arbitrary",), vmem_limit_bytes=VMEM_LIMIT),
        name="ffn",
    )(x1, p, *vec.arrays, *dense)


def _s5_params(lam_re, lam_im, log_dt, b_re, b_im, c_re, c_im):
    g, n = lam_re.shape
    p = b_re.shape[2]
    hg = g // 2
    dt = jnp.exp(log_dt)[:, None]
    mag = jnp.exp(lam_re * dt)
    ar = mag * jnp.cos(lam_im * dt)
    ai = mag * jnp.sin(lam_im * dt)
    den = lam_re * lam_re + lam_im * lam_im
    nr = ar - 1.0
    fr = (nr * lam_re + ai * lam_im) / den
    fi = (ai * lam_re - nr * lam_im) / den
    bbr = fr[..., None] * b_re - fi[..., None] * b_im
    bbi = fr[..., None] * b_im + fi[..., None] * b_re
    def b_layout(b):
        return b.transpose(0, 2, 1).reshape(2, hg * p, n)

    def c_layout(c):
        return c.reshape(2, hg * p, n)

    return (ar, ai,
            b_layout(bbr), b_layout(bbi), c_layout(c_re), c_layout(c_im))


def _lru_gate_layout(w):
    heads, hd, _ = w.shape
    per = MXU_DIM // hd
    return w.reshape(heads // per, per * hd, hd)


def kernel(x, p, g_mix, w_in, b_in, lam_re, lam_im, log_dt, s5_b_re, s5_b_im, s5_c_re, s5_c_im, s5_d, w_glu, b_glu, conv_w, conv_b, w_r, b_r, w_i, b_i, lru_lambda, w_a_out, w_b_out, w_o, g_ffn, w_ffn_gate, w_ffn_up, w_ffn_down, g_ple_gate, w_ple_gate, b_ple_gate, w_ple, g_ple, g_final):
    bsz, seq, d = x.shape
    assert w_in.shape[0] == 1
    assert bsz == SUBLANES and seq % TQ == 0 and seq // TQ >= 2
    assert seq % FFN_ROWS == 0

    a_re, a_im, bre, bim, cre, cim = _s5_params(
        lam_re[0], lam_im[0], log_dt[0], s5_b_re[0], s5_b_im[0],
        s5_c_re[0], s5_c_im[0])

    def row(a):
        return a.reshape(1, -1)

    mixer_vec = _Vectors([
        ("g_mix", g_mix), ("b_in", b_in), ("a_re", a_re), ("a_im", a_im),
        ("s5_d", s5_d[0]), ("b_glu", b_glu), ("conv_w", conv_w[0]),
        ("conv_b", conv_b), ("b_r", b_r[0]), ("b_i", b_i[0]),
        ("lru_lambda", lru_lambda)],
        flatten=("a_re", "a_im", "s5_d", "b_r", "b_i"))
    x1 = _mixer_call(x, mixer_vec, (w_in, w_glu, w_a_out, w_b_out, w_o),
                     (bre, bim, cre, cim,
                      _lru_gate_layout(w_r[0]), _lru_gate_layout(w_i[0])))
    ffn_vec = _Vectors([
        ("g_ffn", g_ffn), ("g_ple_gate", g_ple_gate), ("b_ple_gate", b_ple_gate),
        ("g_ple", g_ple), ("g_final", row(g_final))])
    return _ffn_call(x1, p, ffn_vec,
                     (w_ffn_gate, w_ffn_up, w_ffn_down, w_ple_gate, w_ple))
```
